```python
import math
import jax, jax.numpy as jnp
from jax import lax
import numpy as np

D_MODEL = 2048
BATCH = 4
SEQ = 4096
DEPTH = 2

HEAD_DIM = 64
A_HEADS = 16
DILATED_BRANCHES = ((128, 1), (512, 4), (2048, 16))
B_HEADS = 16
B_KV_HEADS = 2
B_WINDOW = 128
C_HEADS = 16
N_EXPERTS = 64
N_GROUPS = 8
TOPK_GROUPS = 4
TOP_K = 8
EXPERT_DIM = 512
SHARED_DIM = 512
ROUTED_SCALE = 2.5
BAND_BLK = 128
Q_BLK = 128
MOE_BLK = 128
RMS_EPS = 1e-6
NEG_INF = -1e30
AB_IN_COLS = 3 * A_HEADS * HEAD_DIM + (B_HEADS + 2 * B_KV_HEADS) * HEAD_DIM
AB_OUT_ROWS = (A_HEADS + B_HEADS) * HEAD_DIM
C_IN_COLS = 3 * C_HEADS * 2 * HEAD_DIM
C_OUT_ROWS = C_HEADS * 2 * HEAD_DIM

kernel_name = 'hybrid_dilated_swa_diffattn_moe_trunk'


def rmsnorm(x, g):
    x32 = x.astype(jnp.float32)
    y = x32 * lax.rsqrt(jnp.mean(x32 * x32, axis=-1, keepdims=True) + RMS_EPS)
    return (y * g.astype(jnp.float32)).astype(x.dtype)


def head_rmsnorm(x, g):
    x32 = x.astype(jnp.float32)
    return x32 * lax.rsqrt(jnp.mean(x32 * x32, axis=-1, keepdims=True) + RMS_EPS) * g.astype(jnp.float32)


def alibi_slopes(n):
    return jnp.asarray(2.0 ** (-8.0 * np.arange(1, n + 1) / n), dtype=jnp.float32)


def modulate(h, shift, scale):
    return h * (1 + scale) + shift


def banded_attention(q, k, v, slopes, max_dist, pos_scale, sinks):
    Bq, Hq, L, hd = q.shape
    Hkv = k.shape[1]
    G = Hq // Hkv
    nb = -(-L // BAND_BLK)
    Lp = nb * BAND_BLK
    pad = ((0, 0), (0, 0), (0, Lp - L), (0, 0))
    q, k, v = [jnp.pad(a.astype(jnp.float32), pad) for a in (q, k, v)]
    qb = q.reshape(Bq, Hkv, G, nb, BAND_BLK, hd)
    def with_prev(a):
        a = a.reshape(Bq, Hkv, nb, BAND_BLK, hd)
        prev = jnp.pad(a[:, :, :-1], ((0, 0), (0, 0), (1, 0), (0, 0), (0, 0)))
        return jnp.concatenate([prev, a], axis=3)
    kc, vc = with_prev(k), with_prev(v)
    s = jnp.einsum('bhgnqd,bhnkd->bhgnqk', qb, kc) * (hd ** -0.5)
    qi = jnp.arange(BAND_BLK)[:, None]
    ki = jnp.arange(2 * BAND_BLK)[None, :]
    dist = BAND_BLK + qi - ki
    first = (jnp.arange(nb) == 0)[:, None, None] & (ki < BAND_BLK)[None]
    valid = (dist >= 0) & (dist <= max_dist) & (~first)
    bias = -slopes.reshape(Hkv, G)[:, :, None, None, None] * (dist.astype(jnp.float32) * pos_scale)
    s = jnp.where(valid, s + bias, NEG_INF)
    m = jnp.max(s, axis=-1)
    if sinks is not None:
        sink = sinks.astype(jnp.float32).reshape(Hkv, G)[:, :, None, None]
        m = jnp.maximum(m, sink)
    p = jnp.exp(s - m[..., None])
    denom = jnp.sum(p, axis=-1)
    if sinks is not None:
        denom = denom + jnp.exp(sink - m)
    o = jnp.einsum('bhgnqk,bhnkd->bhgnqd', p, vc) / denom[..., None]
    lse = m + jnp.log(denom)
    o = o.reshape(Bq, Hq, Lp, hd)[:, :, :L]
    lse = lse.reshape(Bq, Hq, Lp)[:, :, :L]
    return o, lse


def dilated_mixture(q, k, v, slopes):
    B, H, T, hd = q.shape
    outs, lses = [], []
    for window, dil in DILATED_BRANCHES:
        L = T // dil
        def to_sub(a):
            return a.reshape(B, H, L, dil, hd).transpose(0, 3, 1, 2, 4).reshape(B * dil, H, L, hd)
        o, lse = banded_attention(to_sub(q), to_sub(k), to_sub(v), slopes, window // dil, dil, None)
        outs.append(o.reshape(B, dil, H, L, hd).transpose(0, 2, 3, 1, 4).reshape(B, H, T, hd))
        lses.append(lse.reshape(B, dil, H, L).transpose(0, 2, 3, 1).reshape(B, H, T))
    w = jax.nn.softmax(jnp.stack(lses, axis=0), axis=0)
    return jnp.sum(w[..., None] * jnp.stack(outs, axis=0), axis=0)


def even_mixer(h, w_in, qk_norm, sinks, w_out):
    B, T, _ = h.shape
    a_w = A_HEADS * HEAD_DIM
    b_w = B_HEADS * HEAD_DIM
    kv_w = B_KV_HEADS * HEAD_DIM
    cuts = [a_w, 2 * a_w, 3 * a_w, 3 * a_w + b_w, 3 * a_w + b_w + kv_w]
    qa, ka, va, qb, kb, vb = jnp.split(h @ w_in, cuts, axis=-1)
    def heads(a, n):
        return a.reshape(B, T, n, HEAD_DIM).transpose(0, 2, 1, 3)
    qa = head_rmsnorm(heads(qa, A_HEADS), qk_norm[0])
    ka = head_rmsnorm(heads(ka, A_HEADS), qk_norm[1])
    qb = head_rmsnorm(heads(qb, B_HEADS), qk_norm[2])
    kb = head_rmsnorm(heads(kb, B_KV_HEADS), qk_norm[3])
    oa = dilated_mixture(qa, ka, heads(va, A_HEADS).astype(jnp.float32), alibi_slopes(A_HEADS))
    ob, _ = banded_attention(qb, kb, heads(vb, B_KV_HEADS), alibi_slopes(B_HEADS), B_WINDOW - 1, 1, sinks)
    o = jnp.concatenate([oa, ob], axis=1).transpose(0, 2, 1, 3).reshape(B, T, AB_OUT_ROWS)
    return o.astype(h.dtype) @ w_out


def diff_attention(q1, q2, k1, k2, v, slopes, lam):
    B, H, T, hd = q1.shape
    nb = T // Q_BLK
    scale = hd ** -0.5
    kpos = jnp.arange(T)
    def to_blocks(a):
        return a.reshape(B, H, nb, Q_BLK, hd).transpose(2, 0, 1, 3, 4)
    def one_block(args):
        q1b, q2b, i = args
        qpos = i * Q_BLK + jnp.arange(Q_BLK)
        dist = (qpos[:, None] - kpos[None, :]).astype(jnp.float32)
        causal = dist >= 0
        bias = -slopes[:, None, None] * dist
        def probs(qb, kb):
            s = jnp.einsum('bhqd,bhkd->bhqk', qb, kb) * scale + bias
            return jax.nn.softmax(jnp.where(causal, s, NEG_INF), axis=-1)
        attn = probs(q1b, k1) - lam * probs(q2b, k2)
        return jnp.einsum('bhqk,bhkd->bhqd', attn, v)
    o = lax.map(one_block, (to_blocks(q1), to_blocks(q2), jnp.arange(nb)))
    return o.transpose(1, 2, 0, 3, 4).reshape(B, H, T, 2 * hd)


def odd_mixer(h, w_in, qk_norm, lam_params, subln, w_out, layer_idx):
    B, T, _ = h.shape
    q, k, v = jnp.split(h @ w_in, 3, axis=-1)
    def pairs(a):
        return a.reshape(B, T, C_HEADS, 2, HEAD_DIM).transpose(3, 0, 2, 1, 4)
    q = head_rmsnorm(pairs(q), qk_norm[0])
    k = head_rmsnorm(pairs(k), qk_norm[1])
    v = v.reshape(B, T, C_HEADS, 2 * HEAD_DIM).transpose(0, 2, 1, 3).astype(jnp.float32)
    lam_init = 0.8 - 0.6 * math.exp(-0.3 * layer_idx)
    lp = lam_params.astype(jnp.float32)
    lam = jnp.exp(jnp.sum(lp[0] * lp[1])) - jnp.exp(jnp.sum(lp[2] * lp[3])) + lam_init
    o = diff_attention(q[0], q[1], k[0], k[1], v, alibi_slopes(C_HEADS), lam)
    o = head_rmsnorm(o, subln) * (1.0 - lam_init)
    o = o.transpose(0, 2, 1, 3).reshape(B, T, C_OUT_ROWS)
    return o.astype(h.dtype) @ w_out


def swiglu(x, w_up, w_down):
    g, u = jnp.split(x @ w_up, 2, axis=-1)
    return (jax.nn.silu(g) * u) @ w_down


def routed_experts(tok, idx, gates, experts_up, experts_down):
    n_tok, d = tok.shape
    n_assign = n_tok * TOP_K
    flat_e = idx.reshape(-1).astype(jnp.int32)
    flat_t = jnp.repeat(jnp.arange(n_tok, dtype=jnp.int32), TOP_K)
    flat_w = gates.reshape(-1)
    order = jnp.argsort(flat_e)
    e_sorted = flat_e[order]
    counts = jnp.bincount(flat_e, length=N_EXPERTS).astype(jnp.int32)
    padded = (counts + MOE_BLK - 1) // MOE_BLK * MOE_BLK
    pad_end = jnp.cumsum(padded)
    pad_start = pad_end - padded
    start = jnp.cumsum(counts) - counts
    dest = pad_start[e_sorted] + jnp.arange(n_assign, dtype=jnp.int32) - start[e_sorted]
    n_blocks = -(-(n_assign + N_EXPERTS * (MOE_BLK - 1)) // MOE_BLK)
    cap = n_blocks * MOE_BLK
    buf_tok = jnp.full((cap,), n_tok, jnp.int32).at[dest].set(flat_t[order])
    buf_w = jnp.zeros((cap,), flat_w.dtype).at[dest].set(flat_w[order])
    block_starts = jnp.arange(n_blocks, dtype=jnp.int32) * MOE_BLK
    block_expert = jnp.minimum(jnp.searchsorted(pad_end, block_starts, side='right'), N_EXPERTS - 1)
    tok_pad = jnp.concatenate([tok, jnp.zeros((1, d), tok.dtype)], axis=0)
    def expert_block(args):
        rows, e = args
        return swiglu(tok_pad[rows], experts_up[e], experts_down[e])
    ys = lax.map(expert_block, (buf_tok.reshape(n_blocks, MOE_BLK), block_expert))
    ys = ys.reshape(cap, d) * buf_w[:, None].astype(tok.dtype)
    return jnp.zeros((n_tok + 1, d), tok.dtype).at[buf_tok].add(ys)[:n_tok]


def moe_ffn(h, w_router, router_bias, experts_up, experts_down, shared_up, shared_down):
    B, T, D = h.shape
    tok = h.reshape(B * T, D)
    n_tok = tok.shape[0]
    scores = jax.nn.sigmoid(tok.astype(jnp.float32) @ w_router.astype(jnp.float32))
    sel = scores + router_bias.astype(jnp.float32)
    per_group = N_EXPERTS // N_GROUPS
    grp_score = jnp.sum(lax.top_k(sel.reshape(n_tok, N_GROUPS, per_group), 2)[0], axis=-1)
    _, top_grp = lax.top_k(grp_score, TOPK_GROUPS)
    grp_mask = jnp.any(top_grp[:, :, None] == jnp.arange(N_GROUPS)[None, None, :], axis=1)
    sel = jnp.where(jnp.repeat(grp_mask, per_group, axis=1), sel, NEG_INF)
    _, idx = lax.top_k(sel, TOP_K)
    gates = jnp.take_along_axis(scores, idx, axis=1)
    gates = gates / jnp.sum(gates, axis=-1, keepdims=True) * ROUTED_SCALE
    routed = routed_experts(tok, idx, gates, experts_up, experts_down)
    shared = swiglu(tok, shared_up, shared_down)
    return (routed + shared).reshape(B, T, D)


def setup_inputs(seed: int = 0) -> dict:
    key = jax.random.key(seed)
    ks = jax.random.split(key, 21)
    n_even = (DEPTH + 1) // 2
    n_odd = DEPTH // 2
    f32 = jnp.float32
    def nrm(k, shape, scale):
        return jax.random.normal(k, shape, f32) * scale
    def gain(k, shape):
        return 1.0 + 0.1 * jax.random.normal(k, shape, f32)
    return {
        'x': nrm(ks[0], (BATCH, SEQ, D_MODEL), 1.0),
        'c': nrm(ks[1], (BATCH, D_MODEL), 1.0),
        'norm_mix': gain(ks[2], (DEPTH, D_MODEL)),
        'norm_ffn': gain(ks[3], (DEPTH, D_MODEL)),
        'w_ada': nrm(ks[4], (DEPTH, D_MODEL, 6 * D_MODEL), 0.5 * D_MODEL ** -0.5),
        'b_ada': nrm(ks[5], (DEPTH, 6 * D_MODEL), 0.02),
        'ab_w_in': nrm(ks[6], (n_even, D_MODEL, AB_IN_COLS), D_MODEL ** -0.5),
        'ab_qk_norm': gain(ks[7], (n_even, 4, HEAD_DIM)),
        'ab_sinks': nrm(ks[8], (n_even, B_HEADS), 1.0),
        'ab_w_out': nrm(ks[9], (n_even, AB_OUT_ROWS, D_MODEL), AB_OUT_ROWS ** -0.5),
        'c_w_in': nrm(ks[10], (n_odd, D_MODEL, C_IN_COLS), D_MODEL ** -0.5),
        'c_qk_norm': gain(ks[11], (n_odd, 2, HEAD_DIM)),
        'c_lambda': nrm(ks[12], (n_odd, 4, HEAD_DIM), 0.1),
        'c_subln': gain(ks[13], (n_odd, 2 * HEAD_DIM)),
        'c_w_out': nrm(ks[14], (n_odd, C_OUT_ROWS, D_MODEL), C_OUT_ROWS ** -0.5),
        'w_router': nrm(ks[15], (DEPTH, D_MODEL, N_EXPERTS), D_MODEL ** -0.5),
        'router_bias': nrm(ks[16], (DEPTH, N_EXPERTS), 0.01),
        'experts_up': nrm(ks[17], (DEPTH, N_EXPERTS, D_MODEL, 2 * EXPERT_DIM), D_MODEL ** -0.5),
        'experts_down': nrm(ks[18], (DEPTH, N_EXPERTS, EXPERT_DIM, D_MODEL), EXPERT_DIM ** -0.5),
        'shared_up': nrm(ks[19], (DEPTH, D_MODEL, 2 * SHARED_DIM), D_MODEL ** -0.5),
        'shared_down': nrm(ks[20], (DEPTH, SHARED_DIM, D_MODEL), SHARED_DIM ** -0.5),
    }


def reference(x, c, norm_mix, norm_ffn, w_ada, b_ada, ab_w_in, ab_qk_norm, ab_sinks, ab_w_out,
              c_w_in, c_qk_norm, c_lambda, c_subln, c_w_out, w_router, router_bias,
              experts_up, experts_down, shared_up, shared_down):
    cond = jax.nn.silu(c)
    for i in range(DEPTH):
        ada = cond @ w_ada[i] + b_ada[i]
        sh_m, sc_m, g_m, sh_f, sc_f, g_f = [a[:, None, :] for a in jnp.split(ada, 6, axis=-1)]
        h = modulate(rmsnorm(x, norm_mix[i]), sh_m, sc_m)
        j = i // 2
        if i % 2 == 0:
            mix = even_mixer(h, ab_w_in[j], ab_qk_norm[j], ab_sinks[j], ab_w_out[j])
        else:
            mix = odd_mixer(h, c_w_in[j], c_qk_norm[j], c_lambda[j], c_subln[j], c_w_out[j], i)
        x = x + g_m * mix
        h = modulate(rmsnorm(x, norm_ffn[i]), sh_f, sc_f)
        x = x + g_f * moe_ffn(h, w_router[i], router_bias[i], experts_up[i], experts_down[i],
                              shared_up[i], shared_down[i])
    return x
```

```python
import functools
import math

import jax
import jax.numpy as jnp
import numpy as np
from jax import lax
from jax.experimental import pallas as pl
from jax.experimental.pallas import tpu as pltpu

F32 = jnp.float32
BF16 = jnp.bfloat16
I32 = jnp.int32
HIGHEST = lax.Precision.HIGHEST

HEAD_DIM = 64
A_HEADS = 16
DILATIONS = (1, 4, 16)
A_MAX_DIST = 128
B_HEADS = 16
B_KV_HEADS = 2
B_MAX_DIST = 127
C_HEADS = 16
N_EXPERTS = 64
N_GROUPS = 8
TOPK_GROUPS = 4
TOP_K = 8
EXPERT_DIM = 512
ROUTED_SCALE = 2.5
RMS_EPS = 1e-6
NEG_INF = -1e30

LANES = 128
VMEM_LIMIT = 56 * 1024 * 1024

PROJ_TM = 512
PROJ_TN = 512
ATT_TILE = 2048
BAND = 128
DIFF_TQ = 512
DIFF_TK = 512
ROUTE_TM = 512
MOE_BM = 512
COMB_TM = 128


def _cparams(sem):
    return pltpu.CompilerParams(dimension_semantics=sem, vmem_limit_bytes=VMEM_LIMIT)


def _silu(x):
    return x * jax.nn.sigmoid(x)


def _ada_kernel(c_ref, w_ref, b_ref, o_ref):
    cond = _silu(c_ref[...])
    o_ref[...] = jnp.dot(cond, w_ref[...], precision=HIGHEST,
                         preferred_element_type=F32) + b_ref[...]


def _ada(c, w_ada, b_ada):
    depth, d, d6 = w_ada.shape
    b = c.shape[0]
    bp = 8
    cp = jnp.zeros((bp, d), F32).at[:b].set(c)
    tn = 1024
    out = pl.pallas_call(
        _ada_kernel,
        grid=(depth, d6 // tn),
        in_specs=[
            pl.BlockSpec((bp, d), lambda i, j: (0, 0)),
            pl.BlockSpec((None, d, tn), lambda i, j: (i, 0, j)),
            pl.BlockSpec((None, 1, tn), lambda i, j: (i, 0, j)),
        ],
        out_specs=pl.BlockSpec((None, bp, tn), lambda i, j: (i, 0, j)),
        out_shape=jax.ShapeDtypeStruct((depth, bp, d6), F32),
        compiler_params=_cparams(("arbitrary", "arbitrary")),
        name="ada",
    )(cp, w_ada, b_ada.reshape(depth, 1, d6))
    return out[:, :b]


def _norm_modulate(x, g, sh, sc):
    ms = jnp.mean(x * x, axis=-1, keepdims=True)
    y = x * lax.rsqrt(ms + RMS_EPS) * g
    return y * (1.0 + sc) + sh


def _proj_kernel(x_ref, g_ref, sh_ref, sc_ref, w_ref, hg_ref, pm_ref, o_ref, h_scr, *,
                 n_norm_cols, tn):
    j = pl.program_id(1)

    @pl.when(j == 0)
    def _():
        h = _norm_modulate(x_ref[...], g_ref[...], sh_ref[0], sc_ref[0])
        h_scr[...] = h.astype(BF16)

    acc = jnp.dot(h_scr[...], w_ref[...], preferred_element_type=F32)
    for cb in range(tn // LANES):
        a = acc[:, cb * LANES:(cb + 1) * LANES]
        is_norm = j * tn + cb * LANES < n_norm_cols

        @pl.when(is_norm)
        def _():
            sq = a * a
            hi = sq.astype(BF16)
            lo = (sq - hi.astype(F32)).astype(BF16)
            ms = (jnp.dot(hi, pm_ref[...], preferred_element_type=F32)
                  + jnp.dot(lo, pm_ref[...], preferred_element_type=F32))
            y = a * lax.rsqrt(ms + RMS_EPS) * hg_ref[:, cb * LANES:(cb + 1) * LANES]
            o_ref[cb] = y.astype(BF16)

        @pl.when(jnp.logical_not(is_norm))
        def _():
            o_ref[cb] = a.astype(BF16)


def _head_mean_matrix():
    r = np.arange(LANES)
    pm = (r[:, None] // HEAD_DIM == r[None, :] // HEAD_DIM).astype(np.float32) / HEAD_DIM
    return jnp.asarray(pm, BF16)


def _proj(x2, seq, gain, shift, scale, w, head_gain, n_norm_cols):
    n, d = x2.shape
    cols = w.shape[1]
    tm, tn = PROJ_TM, PROJ_TN
    bsz = shift.shape[0]
    kern = functools.partial(_proj_kernel, n_norm_cols=n_norm_cols, tn=tn)
    return pl.pallas_call(
        kern,
        grid=(n // tm, cols // tn),
        in_specs=[
            pl.BlockSpec((tm, d), lambda i, j: (i, 0)),
            pl.BlockSpec((1, d), lambda i, j: (0, 0)),
            pl.BlockSpec((1, 1, d), lambda i, j: (i * tm // seq, 0, 0)),
            pl.BlockSpec((1, 1, d), lambda i, j: (i * tm // seq, 0, 0)),
            pl.BlockSpec((d, tn), lambda i, j: (0, j)),
            pl.BlockSpec((1, tn), lambda i, j: (0, j)),
            pl.BlockSpec((LANES, LANES), lambda i, j: (0, 0)),
        ],
        out_specs=pl.BlockSpec((tn // LANES, tm, LANES), lambda i, j: (j, i, 0)),
        out_shape=jax.ShapeDtypeStruct((cols // LANES, n, LANES), BF16),
        scratch_shapes=[pltpu.VMEM((tm, d), BF16)],
        compiler_params=_cparams(("arbitrary", "arbitrary")),
        name="proj",
    )(x2, gain.reshape(1, d), shift.reshape(bsz, 1, d), scale.reshape(bsz, 1, d), w,
      head_gain.reshape(1, cols), _head_mean_matrix())


def _band_block(qz, kk, v_aug, bias, sink=None):
    s = lax.dot_general(qz, kk, (((1,), (1,)), ((), ())), preferred_element_type=F32) + bias
    m = jnp.max(s, axis=-1, keepdims=True)
    if sink is not None:
        m = jnp.maximum(m, sink)
    p = jnp.exp(s - m)
    res = jnp.dot(p.astype(BF16), v_aug, preferred_element_type=F32)
    return res, m


def _band_masks(max_dist):
    qi = lax.broadcasted_iota(I32, (BAND, 2 * BAND), 0)
    kj = lax.broadcasted_iota(I32, (BAND, 2 * BAND), 1)
    dist = BAND + qi - kj
    valid = (dist >= 0) & (dist <= max_dist)
    valid_first = valid & (kj >= BAND)
    return dist.astype(F32), valid, valid_first


def _attn_a_kernel(slopes_ref,
                   q1, q4, q16,
                   k1c, k1p, k4c, k4p, k16c, k16p,
                   v1c, v1p, v4c, v4p, v16c, v16p,
                   o_ref, kcat, vcat, st):
    p = pl.program_id(1)
    n = pl.program_id(2)
    lane = lax.broadcasted_iota(I32, (BAND, LANES), 1)
    distf, valid, valid_first = _band_masks(A_MAX_DIST)
    seq_start = n == 0
    branches = ((1, q1, k1c, k1p, v1c, v1p), (4, q4, k4c, k4p, v4c, v4p),
                (16, q16, k16c, k16p, v16c, v16p))
    for bi, (d, qv, kc, kp, vc, vp) in enumerate(branches):
        n_l = ATT_TILE // BAND // d
        rows = n_l * BAND
        biases = []
        for h in range(2):
            slope = slopes_ref[2 * p + h] * float(d)
            b_full = jnp.where(valid, -slope * distf, NEG_INF)
            b_first = jnp.where(valid_first, -slope * distf, NEG_INF)
            biases.append((b_full, jnp.where(seq_start, b_first, b_full)))
        for r in range(d):
            cs = slice(r * LANES, (r + 1) * LANES)
            kcat[0:BAND, :] = kp[:, cs]
            kcat[BAND:BAND + rows, :] = kc[:, cs]
            vprev = vp[:, cs]
            vcur = vc[:, cs]
            for h in range(2):
                vcat[h, 0:BAND, :] = jnp.where(lane // HEAD_DIM == h, vprev, 1.0).astype(BF16)
                lane_k = lax.broadcasted_iota(I32, (rows, LANES), 1)
                vcat[h, BAND:BAND + rows, :] = jnp.where(
                    lane_k // HEAD_DIM == h, vcur, 1.0).astype(BF16)
            for h in range(2):
                m_lane = LANES - 1 if h == 0 else 0
                for jb in range(n_l):
                    qb = qv[jb * BAND:(jb + 1) * BAND, cs]
                    qz = jnp.where(lane // HEAD_DIM == h, qb, 0.0).astype(BF16)
                    kk = kcat[jb * BAND:jb * BAND + 2 * BAND, :]
                    vv = vcat[h, jb * BAND:jb * BAND + 2 * BAND, :]
                    bias = biases[h][1] if jb == 0 else biases[h][0]
                    res, m = _band_block(qz, kk, vv, bias)
                    res = jnp.where(lane == m_lane, m, res)
                    if d == 1:
                        st[bi, h, jb * BAND:(jb + 1) * BAND, :] = res
                    else:
                        st[bi, h, pl.ds(jb * BAND * d + r, BAND, stride=d), :] = res
    for ib in range(ATT_TILE // BAND):
        rs = slice(ib * BAND, (ib + 1) * BAND)
        for h in range(2):
            m_lane = LANES - 1 if h == 0 else 0
            l_lane = HEAD_DIM if h == 0 else 1
            hs = slice(h * HEAD_DIM, (h + 1) * HEAD_DIM)
            parts = [st[bi, h, rs, :] for bi in range(3)]
            ms = [x[:, m_lane:m_lane + 1] for x in parts]
            m_all = jnp.maximum(jnp.maximum(ms[0], ms[1]), ms[2])
            num = sum(jnp.exp(mm - m_all) * x for mm, x in zip(ms, parts))
            out = num / num[:, l_lane:l_lane + 1]
            o_ref[rs, hs] = out[:, hs].astype(BF16)


def _attn_a(qkv, bsz, seq, q_cb, k_cb, v_cb, slopes):
    cb = qkv.shape[0]
    views = {d: qkv.reshape(cb, bsz, seq // d, d * LANES) for d in DILATIONS}
    n_pairs = A_HEADS // 2
    tile = ATT_TILE
    in_specs = [pl.BlockSpec(memory_space=pltpu.SMEM)]
    args = [slopes]
    for d in DILATIONS:
        in_specs.append(pl.BlockSpec((None, None, tile // d, d * LANES),
                                     lambda b, p, n: (q_cb + p, b, n, 0)))
        args.append(views[d])
    for base in (k_cb, v_cb):
        for d in DILATIONS:
            per = tile // d // BAND
            in_specs.append(pl.BlockSpec((None, None, tile // d, d * LANES),
                                         lambda b, p, n, base=base: (base + p, b, n, 0)))
            in_specs.append(pl.BlockSpec(
                (None, None, BAND, d * LANES),
                lambda b, p, n, base=base, per=per: (base + p, b, jnp.maximum(n * per - 1, 0), 0)))
            args += [views[d], views[d]]
    return pl.pallas_call(
        _attn_a_kernel,
        grid=(bsz, n_pairs, seq // tile),
        in_specs=in_specs,
        out_specs=pl.BlockSpec((None, tile, LANES), lambda b, p, n: (b, n, p)),
        out_shape=jax.ShapeDtypeStruct((bsz, seq, A_HEADS * HEAD_DIM), BF16),
        scratch_shapes=[
            pltpu.VMEM((BAND + tile, LANES), BF16),
            pltpu.VMEM((2, BAND + tile, LANES), BF16),
            pltpu.VMEM((3, 2, tile, LANES), F32),
        ],
        compiler_params=_cparams(("arbitrary", "arbitrary", "arbitrary")),
        name="attn_dilated",
    )(*args)


def _attn_b_kernel(slopes_ref, sinks_ref, q_ref, kc, kp, vc, vp, o_ref, kcat, vcat):
    p = pl.program_id(1)
    n = pl.program_id(2)
    lane = lax.broadcasted_iota(I32, (BAND, LANES), 1)
    lane_k = lax.broadcasted_iota(I32, (ATT_TILE, LANES), 1)
    distf, valid, valid_first = _band_masks(B_MAX_DIST)
    seq_start = n == 0
    n_l = ATT_TILE // BAND
    kcat[0:BAND, :] = kp[...]
    kcat[BAND:, :] = kc[...]
    vprev = vp[...]
    vcur = vc[...]
    for h in range(2):
        vcat[h, 0:BAND, :] = jnp.where(lane // HEAD_DIM == h, vprev, 1.0).astype(BF16)
        vcat[h, BAND:, :] = jnp.where(lane_k // HEAD_DIM == h, vcur, 1.0).astype(BF16)
    for h in range(2):
        slope = slopes_ref[2 * p + h]
        sink = sinks_ref[2 * p + h]
        b_full = jnp.where(valid, -slope * distf, NEG_INF)
        b_first = jnp.where(seq_start, jnp.where(valid_first, -slope * distf, NEG_INF), b_full)
        l_lane = HEAD_DIM if h == 0 else 0
        for jb in range(n_l):
            qb = q_ref[jb * BAND:(jb + 1) * BAND, :]
            qz = jnp.where(lane // HEAD_DIM == h, qb, 0.0).astype(BF16)
            kk = kcat[jb * BAND:jb * BAND + 2 * BAND, :]
            vv = vcat[h, jb * BAND:jb * BAND + 2 * BAND, :]
            res, m = _band_block(qz, kk, vv, b_first if jb == 0 else b_full, sink=sink)
            denom = res[:, l_lane:l_lane + 1] + jnp.exp(sink - m)
            hs = slice(h * HEAD_DIM, (h + 1) * HEAD_DIM)
            o_ref[jb * BAND:(jb + 1) * BAND, hs] = (res / denom)[:, hs].astype(BF16)


def _attn_b(qkv, bsz, seq, q_cb, k_cb, v_cb, slopes, sinks):
    cb = qkv.shape[0]
    view = qkv.reshape(cb, bsz, seq, LANES)
    n_pairs = B_HEADS // 2
    pairs_per_kv = n_pairs // B_KV_HEADS
    tile = ATT_TILE
    per = tile // BAND

    def cur(base):
        return pl.BlockSpec((None, None, tile, LANES),
                            lambda b, p, n: (base + p // pairs_per_kv, b, n, 0))

    def prev(base):
        return pl.BlockSpec(
            (None, None, BAND, LANES),
            lambda b, p, n: (base + p // pairs_per_kv, b, jnp.maximum(n * per - 1, 0), 0))

    return pl.pallas_call(
        _attn_b_kernel,
        grid=(bsz, n_pairs, seq // tile),
        in_specs=[
            pl.BlockSpec(memory_space=pltpu.SMEM),
            pl.BlockSpec(memory_space=pltpu.SMEM),
            pl.BlockSpec((None, None, tile, LANES), lambda b, p, n: (q_cb + p, b, n, 0)),
            cur(k_cb), prev(k_cb), cur(v_cb), prev(v_cb),
        ],
        out_specs=pl.BlockSpec((None, tile, LANES), lambda b, p, n: (b, n, p)),
        out_shape=jax.ShapeDtypeStruct((bsz, seq, B_HEADS * HEAD_DIM), BF16),
        scratch_shapes=[
            pltpu.VMEM((BAND + tile, LANES), BF16),
            pltpu.VMEM((2, BAND + tile, LANES), BF16),
        ],
        compiler_params=_cparams(("arbitrary", "arbitrary", "arbitrary")),
        name="attn_swa",
    )(slopes, sinks, view, view, view, view, view)


def _attn_c_kernel(slopes_ref, lam_ref, subln_ref, q_ref, k_ref, v_ref, o_ref,
                   m_scr, l_scr, acc_scr, *, lam_init, nk):
    h = pl.program_id(1)
    qi = pl.program_id(2)
    ki = pl.program_id(3)
    tq, tk = DIFF_TQ, DIFF_TK

    @pl.when(ki == 0)
    def _():
        m_scr[...] = jnp.full(m_scr.shape, NEG_INF, F32)
        l_scr[...] = jnp.zeros(l_scr.shape, F32)
        acc_scr[...] = jnp.zeros(acc_scr.shape, F32)

    @pl.when(ki <= qi)
    def _():
        q = q_ref[...]
        k = k_ref[...]
        v = v_ref[...]
        lane = lax.broadcasted_iota(I32, (tq, LANES), 1)
        rel = (lax.broadcasted_iota(I32, (tq, tk), 1) - lax.broadcasted_iota(I32, (tq, tk), 0)
               + (ki - qi) * tk)
        causal = rel <= 0
        bias = slopes_ref[h] * rel.astype(F32)
        for a in range(2):
            qz = jnp.where(lane // HEAD_DIM == a, q, 0.0).astype(BF16)
            s = lax.dot_general(qz, k, (((1,), (1,)), ((), ())), preferred_element_type=F32)
            s = jnp.where(causal, s + bias, NEG_INF)
            m_prev = m_scr[a]
            m_new = jnp.maximum(m_prev, jnp.max(s, axis=-1, keepdims=True))
            alpha = jnp.exp(m_prev - m_new)
            pr = jnp.exp(s - m_new)
            l_scr[a] = alpha * l_scr[a] + jnp.sum(pr, axis=-1, keepdims=True)
            acc_scr[a] = alpha * acc_scr[a] + jnp.dot(pr.astype(BF16), v,
                                                      preferred_element_type=F32)
            m_scr[a] = m_new

    @pl.when(ki == nk - 1)
    def _():
        lp = lam_ref[...]
        lam = (jnp.exp(jnp.sum(lp[0:1] * lp[1:2], axis=-1, keepdims=True))
               - jnp.exp(jnp.sum(lp[2:3] * lp[3:4], axis=-1, keepdims=True)) + lam_init)
        o = acc_scr[0] / l_scr[0] - lam * (acc_scr[1] / l_scr[1])
        ms = jnp.mean(o * o, axis=-1, keepdims=True)
        o = o * lax.rsqrt(ms + RMS_EPS) * subln_ref[...] * (1.0 - lam_init)
        o_ref[...] = o.astype(BF16)


def _attn_c(qkv, bsz, seq, slopes, lam_params, subln, lam_init):
    cb = qkv.shape[0]
    view = qkv.reshape(cb, bsz, seq, LANES)
    tq, tk = DIFF_TQ, DIFF_TK
    nk = seq // tk
    kern = functools.partial(_attn_c_kernel, lam_init=lam_init, nk=nk)
    return pl.pallas_call(
        kern,
        grid=(bsz, C_HEADS, seq // tq, nk),
        in_specs=[
            pl.BlockSpec(memory_space=pltpu.SMEM),
            pl.BlockSpec((4, HEAD_DIM), lambda b, h, i, j: (0, 0)),
            pl.BlockSpec((1, 2 * HEAD_DIM), lambda b, h, i, j: (0, 0)),
            pl.BlockSpec((None, None, tq, LANES), lambda b, h, i, j: (h, b, i, 0)),
            pl.BlockSpec((None, None, tk, LANES),
                         lambda b, h, i, j: (C_HEADS + h, b, jnp.minimum(j, i), 0)),
            pl.BlockSpec((None, None, tk, LANES),
                         lambda b, h, i, j: (2 * C_HEADS + h, b, jnp.minimum(j, i), 0)),
        ],
        out_specs=pl.BlockSpec((None, tq, LANES), lambda b, h, i, j: (b, i, h)),
        out_shape=jax.ShapeDtypeStruct((bsz, seq, C_HEADS * 2 * HEAD_DIM), BF16),
        scratch_shapes=[
            pltpu.VMEM((2, tq, 1), F32),
            pltpu.VMEM((2, tq, 1), F32),
            pltpu.VMEM((2, tq, LANES), F32),
        ],
        compiler_params=_cparams(("arbitrary", "arbitrary", "arbitrary", "arbitrary")),
        name="attn_diff",
    )(slopes, lam_params, subln.reshape(1, 2 * HEAD_DIM), view, view, view)


def _outproj_kernel(o1_ref, o2_ref, w1_ref, w2_ref, x_ref, g_ref, out_ref):
    acc = (jnp.dot(o1_ref[...], w1_ref[...], preferred_element_type=F32)
           + jnp.dot(o2_ref[...], w2_ref[...], preferred_element_type=F32))
    out_ref[...] = x_ref[...] + g_ref[0] * acc


def _outproj(o1, cb1, o2, cb2, w, x2, gate, seq):
    n, d = x2.shape
    half = w.shape[0] // 2
    tm, tn = 512, 512
    bsz = gate.shape[0]
    return pl.pallas_call(
        _outproj_kernel,
        grid=(n // tm, d // tn),
        in_specs=[
            pl.BlockSpec((tm, half), lambda i, j: (i, cb1)),
            pl.BlockSpec((tm, half), lambda i, j: (i, cb2)),
            pl.BlockSpec((half, tn), lambda i, j: (0, j)),
            pl.BlockSpec((half, tn), lambda i, j: (1, j)),
            pl.BlockSpec((tm, tn), lambda i, j: (i, j)),
            pl.BlockSpec((1, 1, tn), lambda i, j: (i * tm // seq, 0, j)),
        ],
        out_specs=pl.BlockSpec((tm, tn), lambda i, j: (i, j)),
        out_shape=jax.ShapeDtypeStruct((n, d), F32),
        compiler_params=_cparams(("arbitrary", "arbitrary")),
        name="outproj",
    )(o1, o2, w, w, x2, gate.reshape(bsz, 1, d))


def _route_kernel(x_ref, g_ref, sh_ref, sc_ref, wr_ref, rb_ref,
                  h_ref, e8_ref, pos8_ref, gate8_ref, cnt_ref,
                  carry, tri_tok, tri_exp):
    i = pl.program_id(0)
    tm = ROUTE_TM
    per_group = N_EXPERTS // N_GROUPS

    @pl.when(i == 0)
    def _():
        carry[...] = jnp.zeros(carry.shape, F32)
        r = lax.broadcasted_iota(I32, (tm, tm), 0)
        c = lax.broadcasted_iota(I32, (tm, tm), 1)
        tri_tok[...] = jnp.where(r <= c, 1.0, 0.0).astype(BF16)
        re = lax.broadcasted_iota(I32, (N_EXPERTS, N_EXPERTS), 0)
        ce = lax.broadcasted_iota(I32, (N_EXPERTS, N_EXPERTS), 1)
        tri_exp[...] = jnp.where(ce < re, 1.0, 0.0).astype(BF16)

    h = _norm_modulate(x_ref[...], g_ref[...], sh_ref[0], sc_ref[0])
    h_ref[...] = h
    logits = lax.dot_general(wr_ref[...], h, (((1,), (1,)), ((), ())), precision=HIGHEST,
                             preferred_element_type=F32)
    scores = jax.nn.sigmoid(logits)
    sel = scores + rb_ref[...]

    sel3 = sel.reshape(N_GROUPS, per_group, tm)
    within = lax.broadcasted_iota(I32, sel3.shape, 1)
    max1 = jnp.max(sel3, axis=1, keepdims=True)
    first = jnp.min(jnp.where(sel3 == max1, within, per_group), axis=1, keepdims=True)
    max2 = jnp.max(jnp.where(within == first, -jnp.inf, sel3), axis=1, keepdims=True)
    gscore = max1 + max2

    gidx = lax.broadcasted_iota(I32, gscore.shape, 0)
    grank = jnp.zeros(gscore.shape, I32)
    for g in range(N_GROUPS):
        other = gscore[g:g + 1]
        beats = (other > gscore) | ((other == gscore) & (gidx > g))
        grank = grank + jnp.where(beats, 1, 0)
    keep = grank < TOPK_GROUPS
    masked = jnp.where(keep, sel3, NEG_INF).reshape(N_EXPERTS, tm)

    eidx = lax.broadcasted_iota(I32, (N_EXPERTS, tm), 0)
    erank = jnp.zeros((N_EXPERTS, tm), I32)
    for e in range(N_EXPERTS):
        other = masked[e:e + 1, :]
        beats = (other > masked) | ((other == masked) & (eidx > e))
        erank = erank + jnp.where(beats, 1, 0)
    chosen = erank < TOP_K

    gates = jnp.where(chosen, scores, 0.0)
    gates = gates / jnp.sum(gates, axis=0, keepdims=True) * ROUTED_SCALE

    chosen_b = jnp.where(chosen, 1.0, 0.0).astype(BF16)
    slot = jnp.dot(tri_exp[...], chosen_b, preferred_element_type=F32)
    csum = jnp.dot(chosen_b, tri_tok[...], preferred_element_type=F32)
    pos = carry[:, 0:1] + csum - 1.0
    eidx_f = eidx.astype(F32)
    e_rows, p_rows, g_rows = [], [], []
    for kk in range(TOP_K):
        pick = chosen & (slot == float(kk))
        e_rows.append(jnp.sum(jnp.where(pick, eidx_f, 0.0), axis=0, keepdims=True))
        p_rows.append(jnp.sum(jnp.where(pick, pos, 0.0), axis=0, keepdims=True))
        g_rows.append(jnp.sum(jnp.where(pick, gates, 0.0), axis=0, keepdims=True))
    e8_ref[...] = jnp.concatenate(e_rows, axis=0).astype(I32)
    pos8_ref[...] = jnp.concatenate(p_rows, axis=0).astype(I32)
    gate8_ref[...] = jnp.concatenate(g_rows, axis=0)
    total = carry[...] + csum[:, tm - 1:tm]
    carry[...] = total
    cnt_ref[...] = total


def _route(x2, seq, gain, shift, scale, w_router, router_bias):
    n, d = x2.shape
    tm = ROUTE_TM
    bsz = shift.shape[0]
    return pl.pallas_call(
        _route_kernel,
        grid=(n // tm,),
        in_specs=[
            pl.BlockSpec((tm, d), lambda i: (i, 0)),
            pl.BlockSpec((1, d), lambda i: (0, 0)),
            pl.BlockSpec((1, 1, d), lambda i: (i * tm // seq, 0, 0)),
            pl.BlockSpec((1, 1, d), lambda i: (i * tm // seq, 0, 0)),
            pl.BlockSpec((N_EXPERTS, d), lambda i: (0, 0)),
            pl.BlockSpec((N_EXPERTS, 1), lambda i: (0, 0)),
        ],
        out_specs=[
            pl.BlockSpec((tm, d), lambda i: (i, 0)),
            pl.BlockSpec((TOP_K, tm), lambda i: (0, i)),
            pl.BlockSpec((TOP_K, tm), lambda i: (0, i)),
            pl.BlockSpec((TOP_K, tm), lambda i: (0, i)),
            pl.BlockSpec((N_EXPERTS, LANES), lambda i: (0, 0)),
        ],
        out_shape=[
            jax.ShapeDtypeStruct((n, d), F32),
            jax.ShapeDtypeStruct((TOP_K, n), I32),
            jax.ShapeDtypeStruct((TOP_K, n), I32),
            jax.ShapeDtypeStruct((TOP_K, n), F32),
            jax.ShapeDtypeStruct((N_EXPERTS, LANES), F32),
        ],
        scratch_shapes=[
            pltpu.VMEM((N_EXPERTS, LANES), F32),
            pltpu.VMEM((tm, tm), BF16),
            pltpu.VMEM((N_EXPERTS, N_EXPERTS), BF16),
        ],
        compiler_params=_cparams(("arbitrary",)),
        name="route",
    )(x2, gain.reshape(1, d), shift.reshape(bsz, 1, d), scale.reshape(bsz, 1, d),
      w_router.T, router_bias.reshape(N_EXPERTS, 1))


def _tables_kernel(cnt_ref, e8_ref, pos8_ref, dest_ref, bexp_ref, nused_ref, *, nb_pad):
    counts = cnt_ref[...]
    blocks = jnp.ceil(counts / float(MOE_BM))
    re = lax.broadcasted_iota(I32, (N_EXPERTS, N_EXPERTS), 0)
    ce = lax.broadcasted_iota(I32, (N_EXPERTS, N_EXPERTS), 1)
    tri = jnp.where(ce < re, 1.0, 0.0).astype(BF16)
    start_blk = jnp.dot(tri, blocks.astype(BF16), preferred_element_type=F32)
    end_blk = start_blk + blocks
    e8 = e8_ref[...]
    dest = pos8_ref[...].astype(F32)
    for e in range(N_EXPERTS):
        dest = dest + jnp.where(e8 == e, start_blk[e:e + 1, 0:1] * float(MOE_BM), 0.0)
    dest_ref[...] = dest.astype(I32)
    bidx = lax.broadcasted_iota(I32, (N_EXPERTS, nb_pad), 1).astype(F32)
    owner = jnp.sum(jnp.where(end_blk[:, 0:1] <= bidx, 1.0, 0.0), axis=0, keepdims=True)
    bexp_ref[...] = jnp.minimum(owner, float(N_EXPERTS - 1)).astype(I32)
    nused_ref[...] = end_blk[N_EXPERTS - 1:N_EXPERTS, :].astype(I32)


def _tables(counts, e8, pos8, nb):
    n = e8.shape[1]
    tm = 2048
    nb_pad = -(-nb // LANES) * LANES
    kern = functools.partial(_tables_kernel, nb_pad=nb_pad)
    return pl.pallas_call(
        kern,
        grid=(n // tm,),
        in_specs=[
            pl.BlockSpec((N_EXPERTS, LANES), lambda i: (0, 0)),
            pl.BlockSpec((TOP_K, tm), lambda i: (0, i)),
            pl.BlockSpec((TOP_K, tm), lambda i: (0, i)),
        ],
        out_specs=[
            pl.BlockSpec((TOP_K, tm), lambda i: (0, i)),
            pl.BlockSpec((1, nb_pad), lambda i: (0, 0)),
            pl.BlockSpec((1, LANES), lambda i: (0, 0)),
        ],
        out_shape=[
            jax.ShapeDtypeStruct((TOP_K, n), I32),
            jax.ShapeDtypeStruct((1, nb_pad), I32),
            jax.ShapeDtypeStruct((1, LANES), I32),
        ],
        compiler_params=_cparams(("arbitrary",)),
        name="tables",
    )(counts, e8, pos8)


def _experts_kernel(bexp_ref, nused_ref, idx_hbm, h_hbm, wu_ref, wd_ref, y_ref,
                    xbuf, idx_smem, sem_idx, sem_rows):
    b = pl.program_id(0)
    nu = nused_ref[0]
    slot = b % 2
    bm = MOE_BM

    def idx_copy(blk, sl):
        return pltpu.make_async_copy(idx_hbm.at[blk], idx_smem.at[sl], sem_idx.at[sl])

    def issue_rows(sl):
        def body(r, carry):
            tok = idx_smem[sl, r]
            pltpu.make_async_copy(h_hbm.at[pl.ds(tok, 1), :], xbuf.at[sl, pl.ds(r, 1), :],
                                  sem_rows.at[sl]).start()
            return carry
        lax.fori_loop(0, bm, body, 0, unroll=8)

    @pl.when(b == 0)
    def _():
        idx_copy(0, 0).start()
        idx_copy(0, 0).wait()
        issue_rows(0)

        @pl.when(nu > 1)
        def _():
            idx_copy(1, 1).start()

    @pl.when(b + 1 < nu)
    def _():
        idx_copy(b + 1, 1 - slot).wait()
        issue_rows(1 - slot)

    @pl.when(b + 2 < nu)
    def _():
        idx_copy(b + 2, slot).start()

    @pl.when(b < nu)
    def _():
        pltpu.make_async_copy(h_hbm.at[pl.ds(0, bm), :], xbuf.at[slot], sem_rows.at[slot]).wait()
        x = xbuf[slot].astype(BF16)
        up = jnp.dot(x, wu_ref[...], preferred_element_type=F32)
        act = _silu(up[:, :EXPERT_DIM]) * up[:, EXPERT_DIM:]
        y_ref[...] = jnp.dot(act.astype(BF16), wd_ref[...], preferred_element_type=F32)

    @pl.when(b >= nu)
    def _():
        y_ref[...] = jnp.zeros(y_ref.shape, F32)


def _experts(bexp, nused, idx_blocks, h, w_up, w_down):
    nb, bm = idx_blocks.shape
    n, d = h.shape
    grid_spec = pltpu.PrefetchScalarGridSpec(
        num_scalar_prefetch=2,
        grid=(nb,),
        in_specs=[
            pl.BlockSpec(memory_space=pl.ANY),
            pl.BlockSpec(memory_space=pl.ANY),
            pl.BlockSpec((None, d, 2 * EXPERT_DIM), lambda b, be, nu: (be[b], 0, 0)),
            pl.BlockSpec((None, EXPERT_DIM, d), lambda b, be, nu: (be[b], 0, 0)),
        ],
        out_specs=pl.BlockSpec((bm, d), lambda b, be, nu: (b, 0)),
        scratch_shapes=[
            pltpu.VMEM((2, bm, d), F32),
            pltpu.SMEM((2, bm), I32),
            pltpu.SemaphoreType.DMA((2,)),
            pltpu.SemaphoreType.DMA((2,)),
        ],
    )
    return pl.pallas_call(
        _experts_kernel,
        grid_spec=grid_spec,
        out_shape=jax.ShapeDtypeStruct((nb * bm, d), F32),
        compiler_params=_cparams(("arbitrary",)),
        name="experts",
    )(bexp, nused, idx_blocks, h, w_up, w_down)


def _combine_kernel(tab_hbm, y_hbm, gate_ref, h_ref, x_ref, gf_ref, wsu_ref, wsd_ref, out_ref,
                    ybuf, tab_smem, sem_tab, sem_rows, *, n_tiles):
    i = pl.program_id(0)
    slot = i % 2
    tm = COMB_TM

    def tab_copy(t, sl):
        return pltpu.make_async_copy(tab_hbm.at[t], tab_smem.at[sl], sem_tab.at[sl])

    def issue_rows(sl):
        for kk in range(TOP_K):
            def body(r, carry, kk=kk):
                row = tab_smem[sl, kk * tm + r]
                pltpu.make_async_copy(y_hbm.at[pl.ds(row, 1), :],
                                      ybuf.at[sl, kk, pl.ds(r, 1), :],
                                      sem_rows.at[sl]).start()
                return carry
            lax.fori_loop(0, tm, body, 0, unroll=8)

    @pl.when(i == 0)
    def _():
        tab_copy(0, 0).start()
        tab_copy(0, 0).wait()
        issue_rows(0)
        if n_tiles > 1:
            tab_copy(1, 1).start()

    @pl.when(i + 1 < n_tiles)
    def _():
        tab_copy(i + 1, 1 - slot).wait()
        issue_rows(1 - slot)

    @pl.when(i + 2 < n_tiles)
    def _():
        tab_copy(i + 2, slot).start()

    hb = h_ref[...].astype(BF16)
    up = jnp.dot(hb, wsu_ref[...], preferred_element_type=F32)
    half = wsd_ref.shape[0]
    act = _silu(up[:, :half]) * up[:, half:]
    total = jnp.dot(act.astype(BF16), wsd_ref[...], preferred_element_type=F32)

    for kk in range(TOP_K):
        pltpu.make_async_copy(y_hbm.at[pl.ds(0, tm), :], ybuf.at[slot, kk],
                              sem_rows.at[slot]).wait()
    gate = gate_ref[...]
    routed = gate[:, 0:1] * ybuf[slot, 0]
    for kk in range(1, TOP_K):
        routed = routed + gate[:, kk:kk + 1] * ybuf[slot, kk]
    out_ref[...] = x_ref[...] + gf_ref[0] * (routed + total)


def _combine(tab, y, gate_t, h, x2, g_f, w_su, w_sd, seq):
    n, d = x2.shape
    tm = COMB_TM
    n_tiles = n // tm
    bsz = g_f.shape[0]
    kern = functools.partial(_combine_kernel, n_tiles=n_tiles)
    return pl.pallas_call(
        kern,
        grid=(n_tiles,),
        in_specs=[
            pl.BlockSpec(memory_space=pl.ANY),
            pl.BlockSpec(memory_space=pl.ANY),
            pl.BlockSpec((tm, TOP_K), lambda i: (i, 0)),
            pl.BlockSpec((tm, d), lambda i: (i, 0)),
            pl.BlockSpec((tm, d), lambda i: (i, 0)),
            pl.BlockSpec((1, 1, d), lambda i: (i * tm // seq, 0, 0)),
            pl.BlockSpec(w_su.shape, lambda i: (0, 0)),
            pl.BlockSpec(w_sd.shape, lambda i: (0, 0)),
        ],
        out_specs=pl.BlockSpec((tm, d), lambda i: (i, 0)),
        out_shape=jax.ShapeDtypeStruct((n, d), F32),
        scratch_shapes=[
            pltpu.VMEM((2, TOP_K, tm, d), F32),
            pltpu.SMEM((2, TOP_K * tm), I32),
            pltpu.SemaphoreType.DMA((2,)),
            pltpu.SemaphoreType.DMA((2,)),
        ],
        compiler_params=_cparams(("arbitrary",)),
        name="combine",
    )(tab, y, gate_t, h, x2, g_f.reshape(bsz, 1, d), w_su, w_sd)


def _moe(x2, seq, gain, shift, scale, g_f, w_router, router_bias, w_up, w_down, w_su, w_sd):
    n, d = x2.shape
    h, e8, pos8, gate8, counts = _route(x2, seq, gain, shift, scale, w_router, router_bias)
    nb = -(-(n * TOP_K + N_EXPERTS * (MOE_BM - 1)) // MOE_BM)
    dest8, bexp, nused = _tables(counts, e8, pos8, nb)
    tok_ids = jnp.broadcast_to(jnp.arange(n, dtype=I32)[None, :], (TOP_K, n))
    row_tok = jnp.zeros((nb * MOE_BM,), I32).at[dest8.reshape(-1)].set(tok_ids.reshape(-1))
    y = _experts(bexp[0, :nb], nused[0, :1], row_tok.reshape(nb, MOE_BM), h,
                 w_up.astype(BF16), w_down.astype(BF16))
    tab = dest8.reshape(TOP_K, n // COMB_TM, COMB_TM).transpose(1, 0, 2).reshape(
        n // COMB_TM, TOP_K * COMB_TM)
    return _combine(tab, y, gate8.T, h, x2, g_f, w_su.astype(BF16), w_sd.astype(BF16), seq)


def _alibi_slopes(nh):
    return jnp.asarray(2.0 ** (-8.0 * np.arange(1, nh + 1) / nh), dtype=F32)


def _even_weights(w_in, qk_norm):
    a_w = A_HEADS * HEAD_DIM
    b_w = B_HEADS * HEAD_DIM
    kv_w = B_KV_HEADS * HEAD_DIM
    cuts = np.cumsum([a_w, a_w, a_w, b_w, kv_w])
    qa, ka, va, qb, kb, vb = jnp.split(w_in, cuts, axis=1)

    def dup(a):
        return jnp.concatenate([a[:, hh * HEAD_DIM:(hh + 1) * HEAD_DIM]
                                for hh in range(B_KV_HEADS) for _ in range(2)], axis=1)

    w = jnp.concatenate([qa, ka, qb, dup(kb), va, dup(vb)], axis=1).astype(BF16)
    q_scale = HEAD_DIM ** -0.5
    gains = jnp.concatenate([
        jnp.tile(qk_norm[0] * q_scale, A_HEADS), jnp.tile(qk_norm[1], A_HEADS),
        jnp.tile(qk_norm[2] * q_scale, B_HEADS), jnp.tile(qk_norm[3], 2 * B_KV_HEADS),
        jnp.ones((a_w + 2 * kv_w,), F32)])
    n_norm = 2 * a_w + b_w + 2 * kv_w
    return w, gains, n_norm


def kernel(x, c, norm_mix, norm_ffn, w_ada, b_ada, ab_w_in, ab_qk_norm, ab_sinks, ab_w_out,
           c_w_in, c_qk_norm, c_lambda, c_subln, c_w_out, w_router, router_bias,
           experts_up, experts_down, shared_up, shared_down):
    bsz, seq, d = x.shape
    depth = w_ada.shape[0]
    n = bsz * seq
    ada = _ada(c, w_ada, b_ada)
    x2 = x.reshape(n, d)
    for i in range(depth):
        sh_m, sc_m, g_m, sh_f, sc_f, g_f = jnp.split(ada[i], 6, axis=-1)
        j = i // 2
        if i % 2 == 0:
            w, gains, n_norm = _even_weights(ab_w_in[j], ab_qk_norm[j])
            qkv = _proj(x2, seq, norm_mix[i], sh_m, sc_m, w, gains, n_norm)
            pa = A_HEADS // 2
            pb = B_HEADS // 2
            q_a, k_a, q_b, k_b = 0, pa, 2 * pa, 2 * pa + pb
            v_a = k_b + B_KV_HEADS
            v_b = v_a + pa
            o_a = _attn_a(qkv, bsz, seq, q_a, k_a, v_a, _alibi_slopes(A_HEADS))
            o_b = _attn_b(qkv, bsz, seq, q_b, k_b, v_b, _alibi_slopes(B_HEADS),
                          ab_sinks[j].astype(F32))
            x2 = _outproj(o_a.reshape(n, -1), 0, o_b.reshape(n, -1), 0,
                          ab_w_out[j].astype(BF16), x2, g_m, seq)
        else:
            q_scale = HEAD_DIM ** -0.5
            gains = jnp.concatenate([
                jnp.tile(c_qk_norm[j, 0] * q_scale, 2 * C_HEADS),
                jnp.tile(c_qk_norm[j, 1], 2 * C_HEADS),
                jnp.ones((2 * C_HEADS * HEAD_DIM,), F32)])
            qkv = _proj(x2, seq, norm_mix[i], sh_m, sc_m, c_w_in[j].astype(BF16), gains,
                        4 * C_HEADS * HEAD_DIM)
            lam_init = 0.8 - 0.6 * math.exp(-0.3 * i)
            o_c = _attn_c(qkv, bsz, seq, _alibi_slopes(C_HEADS), c_lambda[j].astype(F32),
                          c_subln[j].astype(F32), lam_init).reshape(n, -1)
            x2 = _outproj(o_c, 0, o_c, 1, c_w_out[j].astype(BF16), x2, g_m, seq)
        x2 = _moe(x2, seq, norm_ffn[i], sh_f, sc_f, g_f, w_router[i], router_bias[i],
                  experts_up[i], experts_down[i], shared_up[i], shared_down[i])
    return x2.reshape(bsz, seq, d)
```

```python
import functools
import math

import jax
import jax.numpy as jnp
import numpy as np
from jax import lax
from jax.experimental import pallas as pl
from jax.experimental.pallas import tpu as pltpu

F32 = jnp.float32
BF16 = jnp.bfloat16
I32 = jnp.int32
HIGHEST = lax.Precision.HIGHEST

HEAD_DIM = 64
A_HEADS = 16
DILATIONS = (1, 4, 16)
A_MAX_DIST = 128
B_HEADS = 16
B_KV_HEADS = 2
B_MAX_DIST = 127
C_HEADS = 16
N_EXPERTS = 64
N_GROUPS = 8
TOPK_GROUPS = 4
TOP_K = 8
EXPERT_DIM = 512
ROUTED_SCALE = 2.5
RMS_EPS = 1e-6
NEG_INF = -1e30
LOG2E = math.log2(math.e)
SCORE_BOUND_MARGIN = 1.02
MAX_BOUNDED_EXPONENT = 100.0

LANES = 128
VMEM_LIMIT = 56 * 1024 * 1024

PROJ_TM = 512
PROJ_TN = 512
ATT_TILE = 2048
BAND = 128
DIFF_TQ = 512
DIFF_TK = 512
ROUTE_TM = 512
MOE_BM = 512
COMB_TM = 128


def _cparams(sem):
    return pltpu.CompilerParams(dimension_semantics=sem, vmem_limit_bytes=VMEM_LIMIT)


def _silu(x):
    return x * jax.nn.sigmoid(x)


def _ada_kernel(c_ref, w_ref, b_ref, o_ref):
    cond = _silu(c_ref[...])
    o_ref[...] = jnp.dot(cond, w_ref[...], precision=HIGHEST,
                         preferred_element_type=F32) + b_ref[...]


def _ada(c, w_ada, b_ada):
    depth, d, d6 = w_ada.shape
    b = c.shape[0]
    bp = 8
    cp = jnp.zeros((bp, d), F32).at[:b].set(c)
    tn = 1024
    out = pl.pallas_call(
        _ada_kernel,
        grid=(depth, d6 // tn),
        in_specs=[
            pl.BlockSpec((bp, d), lambda i, j: (0, 0)),
            pl.BlockSpec((None, d, tn), lambda i, j: (i, 0, j)),
            pl.BlockSpec((None, 1, tn), lambda i, j: (i, 0, j)),
        ],
        out_specs=pl.BlockSpec((None, bp, tn), lambda i, j: (i, 0, j)),
        out_shape=jax.ShapeDtypeStruct((depth, bp, d6), F32),
        compiler_params=_cparams(("arbitrary", "arbitrary")),
        name="ada",
    )(cp, w_ada, b_ada.reshape(depth, 1, d6))
    return out[:, :b]


def _norm_modulate(x, g, sh, sc):
    ms = jnp.mean(x * x, axis=-1, keepdims=True)
    y = x * lax.rsqrt(ms + RMS_EPS) * g
    return y * (1.0 + sc) + sh


def _proj_kernel(x_ref, g_ref, sh_ref, sc_ref, w_ref, hg_ref, pm_ref, o_ref, h_scr, *,
                 n_norm_cols, tn):
    j = pl.program_id(1)

    @pl.when(j == 0)
    def _():
        h = _norm_modulate(x_ref[...], g_ref[...], sh_ref[0], sc_ref[0])
        h_scr[...] = h.astype(BF16)

    acc = jnp.dot(h_scr[...], w_ref[...], preferred_element_type=F32)
    for cb in range(tn // LANES):
        a = acc[:, cb * LANES:(cb + 1) * LANES]
        is_norm = j * tn + cb * LANES < n_norm_cols

        @pl.when(is_norm)
        def _():
            sq = a * a
            hi = sq.astype(BF16)
            lo = (sq - hi.astype(F32)).astype(BF16)
            ms = (jnp.dot(hi, pm_ref[...], preferred_element_type=F32)
                  + jnp.dot(lo, pm_ref[...], preferred_element_type=F32))
            y = a * lax.rsqrt(ms + RMS_EPS) * hg_ref[:, cb * LANES:(cb + 1) * LANES]
            o_ref[cb] = y.astype(BF16)

        @pl.when(jnp.logical_not(is_norm))
        def _():
            o_ref[cb] = a.astype(BF16)


def _head_mean_matrix():
    r = np.arange(LANES)
    pm = (r[:, None] // HEAD_DIM == r[None, :] // HEAD_DIM).astype(np.float32) / HEAD_DIM
    return jnp.asarray(pm, BF16)


def _proj(x2, seq, gain, shift, scale, w, head_gain, n_norm_cols):
    n, d = x2.shape
    cols = w.shape[1]
    tm, tn = PROJ_TM, PROJ_TN
    bsz = shift.shape[0]
    kern = functools.partial(_proj_kernel, n_norm_cols=n_norm_cols, tn=tn)
    return pl.pallas_call(
        kern,
        grid=(n // tm, cols // tn),
        in_specs=[
            pl.BlockSpec((tm, d), lambda i, j: (i, 0)),
            pl.BlockSpec((1, d), lambda i, j: (0, 0)),
            pl.BlockSpec((1, 1, d), lambda i, j: (i * tm // seq, 0, 0)),
            pl.BlockSpec((1, 1, d), lambda i, j: (i * tm // seq, 0, 0)),
            pl.BlockSpec((d, tn), lambda i, j: (0, j)),
            pl.BlockSpec((1, tn), lambda i, j: (0, j)),
            pl.BlockSpec((LANES, LANES), lambda i, j: (0, 0)),
        ],
        out_specs=pl.BlockSpec((tn // LANES, tm, LANES), lambda i, j: (j, i, 0)),
        out_shape=jax.ShapeDtypeStruct((cols // LANES, n, LANES), BF16),
        scratch_shapes=[pltpu.VMEM((tm, d), BF16)],
        compiler_params=_cparams(("arbitrary", "arbitrary")),
        name="proj",
    )(x2, gain.reshape(1, d), shift.reshape(bsz, 1, d), scale.reshape(bsz, 1, d), w,
      head_gain.reshape(1, cols), _head_mean_matrix())


def _band_block(qz, kk, v_aug, bias, sink=None):
    s = lax.dot_general(qz, kk, (((1,), (1,)), ((), ())), preferred_element_type=F32) + bias
    m = jnp.max(s, axis=-1, keepdims=True)
    if sink is not None:
        m = jnp.maximum(m, sink)
    p = jnp.exp(s - m)
    res = jnp.dot(p.astype(BF16), v_aug, preferred_element_type=F32)
    return res, m


def _band_masks(max_dist):
    qi = lax.broadcasted_iota(I32, (BAND, 2 * BAND), 0)
    kj = lax.broadcasted_iota(I32, (BAND, 2 * BAND), 1)
    dist = BAND + qi - kj
    valid = (dist >= 0) & (dist <= max_dist)
    valid_first = valid & (kj >= BAND)
    return dist.astype(F32), valid, valid_first


def _attn_a_kernel(slopes_ref,
                   q1, q4, q16,
                   k1c, k1p, k4c, k4p, k16c, k16p,
                   v1c, v1p, v4c, v4p, v16c, v16p,
                   o_ref, kcat, vcat, st):
    p = pl.program_id(1)
    n = pl.program_id(2)
    lane = lax.broadcasted_iota(I32, (BAND, LANES), 1)
    distf, valid, valid_first = _band_masks(A_MAX_DIST)
    seq_start = n == 0
    branches = ((1, q1, k1c, k1p, v1c, v1p), (4, q4, k4c, k4p, v4c, v4p),
                (16, q16, k16c, k16p, v16c, v16p))
    for bi, (d, qv, kc, kp, vc, vp) in enumerate(branches):
        n_l = ATT_TILE // BAND // d
        rows = n_l * BAND
        biases = []
        for h in range(2):
            slope = slopes_ref[2 * p + h] * float(d)
            b_full = jnp.where(valid, -slope * distf, NEG_INF)
            b_first = jnp.where(valid_first, -slope * distf, NEG_INF)
            biases.append((b_full, jnp.where(seq_start, b_first, b_full)))
        for r in range(d):
            cs = slice(r * LANES, (r + 1) * LANES)
            kcat[0:BAND, :] = kp[:, cs]
            kcat[BAND:BAND + rows, :] = kc[:, cs]
            vprev = vp[:, cs]
            vcur = vc[:, cs]
            for h in range(2):
                vcat[h, 0:BAND, :] = jnp.where(lane // HEAD_DIM == h, vprev, 1.0).astype(BF16)
                lane_k = lax.broadcasted_iota(I32, (rows, LANES), 1)
                vcat[h, BAND:BAND + rows, :] = jnp.where(
                    lane_k // HEAD_DIM == h, vcur, 1.0).astype(BF16)
            for h in range(2):
                m_lane = LANES - 1 if h == 0 else 0
                for jb in range(n_l):
                    qb = qv[jb * BAND:(jb + 1) * BAND, cs]
                    qz = jnp.where(lane // HEAD_DIM == h, qb, 0.0).astype(BF16)
                    kk = kcat[jb * BAND:jb * BAND + 2 * BAND, :]
                    vv = vcat[h, jb * BAND:jb * BAND + 2 * BAND, :]
                    bias = biases[h][1] if jb == 0 else biases[h][0]
                    res, m = _band_block(qz, kk, vv, bias)
                    res = jnp.where(lane == m_lane, m, res)
                    if d == 1:
                        st[bi, h, jb * BAND:(jb + 1) * BAND, :] = res
                    else:
                        st[bi, h, pl.ds(jb * BAND * d + r, BAND, stride=d), :] = res
    for ib in range(ATT_TILE // BAND):
        rs = slice(ib * BAND, (ib + 1) * BAND)
        for h in range(2):
            m_lane = LANES - 1 if h == 0 else 0
            l_lane = HEAD_DIM if h == 0 else 1
            hs = slice(h * HEAD_DIM, (h + 1) * HEAD_DIM)
            parts = [st[bi, h, rs, :] for bi in range(3)]
            ms = [x[:, m_lane:m_lane + 1] for x in parts]
            m_all = jnp.maximum(jnp.maximum(ms[0], ms[1]), ms[2])
            num = sum(jnp.exp(mm - m_all) * x for mm, x in zip(ms, parts))
            out = num / num[:, l_lane:l_lane + 1]
            o_ref[rs, hs] = out[:, hs].astype(BF16)


def _attn_a(qkv, bsz, seq, q_cb, k_cb, v_cb, slopes):
    cb = qkv.shape[0]
    views = {d: qkv.reshape(cb, bsz, seq // d, d * LANES) for d in DILATIONS}
    n_pairs = A_HEADS // 2
    tile = ATT_TILE
    in_specs = [pl.BlockSpec(memory_space=pltpu.SMEM)]
    args = [slopes]
    for d in DILATIONS:
        in_specs.append(pl.BlockSpec((None, None, tile // d, d * LANES),
                                     lambda b, p, n: (q_cb + p, b, n, 0)))
        args.append(views[d])
    for base in (k_cb, v_cb):
        for d in DILATIONS:
            per = tile // d // BAND
            in_specs.append(pl.BlockSpec((None, None, tile // d, d * LANES),
                                         lambda b, p, n, base=base: (base + p, b, n, 0)))
            in_specs.append(pl.BlockSpec(
                (None, None, BAND, d * LANES),
                lambda b, p, n, base=base, per=per: (base + p, b, jnp.maximum(n * per - 1, 0), 0)))
            args += [views[d], views[d]]
    return pl.pallas_call(
        _attn_a_kernel,
        grid=(bsz, n_pairs, seq // tile),
        in_specs=in_specs,
        out_specs=pl.BlockSpec((None, tile, LANES), lambda b, p, n: (b, n, p)),
        out_shape=jax.ShapeDtypeStruct((bsz, seq, A_HEADS * HEAD_DIM), BF16),
        scratch_shapes=[
            pltpu.VMEM((BAND + tile, LANES), BF16),
            pltpu.VMEM((2, BAND + tile, LANES), BF16),
            pltpu.VMEM((3, 2, tile, LANES), F32),
        ],
        compiler_params=_cparams(("arbitrary", "arbitrary", "arbitrary")),
        name="attn_dilated",
    )(*args)


def _attn_b_kernel(slopes_ref, sinks_ref, q_ref, kc, kp, vc, vp, o_ref, kcat, vcat):
    p = pl.program_id(1)
    n = pl.program_id(2)
    lane = lax.broadcasted_iota(I32, (BAND, LANES), 1)
    lane_k = lax.broadcasted_iota(I32, (ATT_TILE, LANES), 1)
    distf, valid, valid_first = _band_masks(B_MAX_DIST)
    seq_start = n == 0
    n_l = ATT_TILE // BAND
    kcat[0:BAND, :] = kp[...]
    kcat[BAND:, :] = kc[...]
    vprev = vp[...]
    vcur = vc[...]
    for h in range(2):
        vcat[h, 0:BAND, :] = jnp.where(lane // HEAD_DIM == h, vprev, 1.0).astype(BF16)
        vcat[h, BAND:, :] = jnp.where(lane_k // HEAD_DIM == h, vcur, 1.0).astype(BF16)
    for h in range(2):
        slope = slopes_ref[2 * p + h]
        sink = sinks_ref[2 * p + h]
        b_full = jnp.where(valid, -slope * distf, NEG_INF)
        b_first = jnp.where(seq_start, jnp.where(valid_first, -slope * distf, NEG_INF), b_full)
        l_lane = HEAD_DIM if h == 0 else 0
        for jb in range(n_l):
            qb = q_ref[jb * BAND:(jb + 1) * BAND, :]
            qz = jnp.where(lane // HEAD_DIM == h, qb, 0.0).astype(BF16)
            kk = kcat[jb * BAND:jb * BAND + 2 * BAND, :]
            vv = vcat[h, jb * BAND:jb * BAND + 2 * BAND, :]
            res, m = _band_block(qz, kk, vv, b_first if jb == 0 else b_full, sink=sink)
            denom = res[:, l_lane:l_lane + 1] + jnp.exp(sink - m)
            hs = slice(h * HEAD_DIM, (h + 1) * HEAD_DIM)
            o_ref[jb * BAND:(jb + 1) * BAND, hs] = (res / denom)[:, hs].astype(BF16)


def _attn_b(qkv, bsz, seq, q_cb, k_cb, v_cb, slopes, sinks):
    cb = qkv.shape[0]
    view = qkv.reshape(cb, bsz, seq, LANES)
    n_pairs = B_HEADS // 2
    pairs_per_kv = n_pairs // B_KV_HEADS
    tile = ATT_TILE
    per = tile // BAND

    def cur(base):
        return pl.BlockSpec((None, None, tile, LANES),
                            lambda b, p, n: (base + p // pairs_per_kv, b, n, 0))

    def prev(base):
        return pl.BlockSpec(
            (None, None, BAND, LANES),
            lambda b, p, n: (base + p // pairs_per_kv, b, jnp.maximum(n * per - 1, 0), 0))

    return pl.pallas_call(
        _attn_b_kernel,
        grid=(bsz, n_pairs, seq // tile),
        in_specs=[
            pl.BlockSpec(memory_space=pltpu.SMEM),
            pl.BlockSpec(memory_space=pltpu.SMEM),
            pl.BlockSpec((None, None, tile, LANES), lambda b, p, n: (q_cb + p, b, n, 0)),
            cur(k_cb), prev(k_cb), cur(v_cb), prev(v_cb),
        ],
        out_specs=pl.BlockSpec((None, tile, LANES), lambda b, p, n: (b, n, p)),
        out_shape=jax.ShapeDtypeStruct((bsz, seq, B_HEADS * HEAD_DIM), BF16),
        scratch_shapes=[
            pltpu.VMEM((BAND + tile, LANES), BF16),
            pltpu.VMEM((2, BAND + tile, LANES), BF16),
        ],
        compiler_params=_cparams(("arbitrary", "arbitrary", "arbitrary")),
        name="attn_swa",
    )(slopes, sinks, view, view, view, view, view)


def _attn_c_kernel(qi_tab, ki_tab, bounded_ref, consts_ref, lam_ref, subln_ref,
                   q_ref, k_ref, v_ref, o_ref, m_scr, l_scr, acc_scr, bias_scr, *, lam_init):
    h = pl.program_id(1)
    step = pl.program_id(2)
    qi = qi_tab[step]
    ki = ki_tab[step]
    tq, tk = DIFF_TQ, DIFF_TK
    slope2 = consts_ref[h] * LOG2E
    bound2 = consts_ref[C_HEADS]
    bounded = bounded_ref[0] == 1
    tile_off = slope2 * ((ki - qi) * tk).astype(F32)

    @pl.when(step == 0)
    def _():
        rel = (lax.broadcasted_iota(I32, (tq, tk), 1)
               - lax.broadcasted_iota(I32, (tq, tk), 0)).astype(F32)
        bias = slope2 * rel
        bias_scr[0] = bias
        bias_scr[1] = jnp.where(rel <= 0.0, bias, NEG_INF)

    @pl.when(ki == 0)
    def _():
        m_scr[...] = jnp.full(m_scr.shape, NEG_INF, F32)
        l_scr[...] = jnp.zeros(l_scr.shape, F32)
        acc_scr[...] = jnp.zeros(acc_scr.shape, F32)

    def scores(a, q, k, lane):
        qz = jnp.where(lane // HEAD_DIM == a, q, 0.0).astype(BF16)
        return lax.dot_general(qz, k, (((1,), (1,)), ((), ())), preferred_element_type=F32)

    def update_online(bias_idx):
        q = q_ref[...]
        k = k_ref[...]
        v = v_ref[...]
        lane = lax.broadcasted_iota(I32, (tq, LANES), 1)
        for a in range(2):
            s = scores(a, q, k, lane) + bias_scr[bias_idx]
            m_prev = m_scr[a]
            m_new = jnp.maximum(m_prev, jnp.max(s, axis=-1, keepdims=True) + tile_off)
            alpha = jnp.exp2(m_prev - m_new)
            pr = jnp.exp2(s - (m_new - tile_off))
            l_scr[a] = alpha * l_scr[a] + jnp.sum(pr, axis=-1, keepdims=True)
            acc_scr[a, :, 0:LANES] = alpha * acc_scr[a, :, 0:LANES] + jnp.dot(
                pr.astype(BF16), v, preferred_element_type=F32)
            m_scr[a] = m_new

    def update_bounded(bias_idx):
        q = q_ref[...]
        k = k_ref[...]
        v = v_ref[...]
        lane = lax.broadcasted_iota(I32, (tq, LANES), 1)
        bias = bias_scr[bias_idx] + (tile_off - bound2)
        v_aug = jnp.concatenate([v, jnp.ones_like(v)], axis=1)
        for a in range(2):
            pr = jnp.exp2(scores(a, q, k, lane) + bias).astype(BF16)
            acc_scr[a] = acc_scr[a] + jnp.dot(pr, v_aug, preferred_element_type=F32)

    def finish(o1, o2):
        lp = lam_ref[...]
        lam = (jnp.exp(jnp.sum(lp[0:1] * lp[1:2], axis=-1, keepdims=True))
               - jnp.exp(jnp.sum(lp[2:3] * lp[3:4], axis=-1, keepdims=True)) + lam_init)
        o = o1 - lam * o2
        ms = jnp.mean(o * o, axis=-1, keepdims=True)
        o = o * lax.rsqrt(ms + RMS_EPS) * subln_ref[...] * (1.0 - lam_init)
        o_ref[...] = o.astype(BF16)

    @pl.when(bounded & (ki < qi))
    def _():
        update_bounded(0)

    @pl.when(bounded & (ki == qi))
    def _():
        update_bounded(1)
        finish(acc_scr[0, :, 0:LANES] / acc_scr[0, :, LANES:],
               acc_scr[1, :, 0:LANES] / acc_scr[1, :, LANES:])

    @pl.when(jnp.logical_not(bounded) & (ki < qi))
    def _():
        update_online(0)

    @pl.when(jnp.logical_not(bounded) & (ki == qi))
    def _():
        update_online(1)
        finish(acc_scr[0, :, 0:LANES] / l_scr[0], acc_scr[1, :, 0:LANES] / l_scr[1])


def _attn_c(qkv, bsz, seq, slopes, qk_gains, lam_params, subln, lam_init):
    cb = qkv.shape[0]
    view = qkv.reshape(cb, bsz, seq, LANES)
    tq, tk = DIFF_TQ, DIFF_TK
    assert tq == tk
    nq = seq // tq
    pairs = [(i, j) for i in range(nq) for j in range(i + 1)]
    qi_tab = jnp.asarray([p[0] for p in pairs], I32)
    ki_tab = jnp.asarray([p[1] for p in pairs], I32)
    g_q = jnp.max(jnp.abs(qk_gains[0].astype(F32)))
    g_k = jnp.max(jnp.abs(qk_gains[1].astype(F32)))
    bound2 = SCORE_BOUND_MARGIN * HEAD_DIM ** 0.5 * g_q * g_k * LOG2E
    bounded = (2.0 * bound2 < MAX_BOUNDED_EXPONENT).astype(I32).reshape(1)
    consts = jnp.concatenate([slopes, bound2.reshape(1)])
    kern = functools.partial(_attn_c_kernel, lam_init=lam_init)

    def im(fn):
        return lambda b, h, s, qt, kt, bd: fn(b, h, s, qt, kt)

    grid_spec = pltpu.PrefetchScalarGridSpec(
        num_scalar_prefetch=3,
        grid=(bsz, C_HEADS, len(pairs)),
        in_specs=[
            pl.BlockSpec(memory_space=pltpu.SMEM),
            pl.BlockSpec((4, HEAD_DIM), im(lambda b, h, s, qt, kt: (0, 0))),
            pl.BlockSpec((1, 2 * HEAD_DIM), im(lambda b, h, s, qt, kt: (0, 0))),
            pl.BlockSpec((None, None, tq, LANES), im(lambda b, h, s, qt, kt: (h, b, qt[s], 0))),
            pl.BlockSpec((None, None, tk, LANES),
                         im(lambda b, h, s, qt, kt: (C_HEADS + h, b, kt[s], 0))),
            pl.BlockSpec((None, None, tk, LANES),
                         im(lambda b, h, s, qt, kt: (2 * C_HEADS + h, b, kt[s], 0))),
        ],
        out_specs=pl.BlockSpec((None, tq, LANES), im(lambda b, h, s, qt, kt: (b, qt[s], h))),
        scratch_shapes=[
            pltpu.VMEM((2, tq, 1), F32),
            pltpu.VMEM((2, tq, 1), F32),
            pltpu.VMEM((2, tq, 2 * LANES), F32),
            pltpu.VMEM((2, tq, tk), F32),
        ],
    )
    return pl.pallas_call(
        kern,
        grid_spec=grid_spec,
        out_shape=jax.ShapeDtypeStruct((bsz, seq, C_HEADS * 2 * HEAD_DIM), BF16),
        compiler_params=_cparams(("arbitrary", "arbitrary", "arbitrary")),
        name="attn_diff",
    )(qi_tab, ki_tab, bounded, consts, lam_params, subln.reshape(1, 2 * HEAD_DIM),
      view, view, view)


def _outproj_kernel(o1_ref, o2_ref, w1_ref, w2_ref, x_ref, g_ref, out_ref):
    acc = (jnp.dot(o1_ref[...], w1_ref[...], preferred_element_type=F32)
           + jnp.dot(o2_ref[...], w2_ref[...], preferred_element_type=F32))
    out_ref[...] = x_ref[...] + g_ref[0] * acc


def _outproj(o1, cb1, o2, cb2, w, x2, gate, seq):
    n, d = x2.shape
    half = w.shape[0] // 2
    tm, tn = 512, 512
    bsz = gate.shape[0]
    return pl.pallas_call(
        _outproj_kernel,
        grid=(n // tm, d // tn),
        in_specs=[
            pl.BlockSpec((tm, half), lambda i, j: (i, cb1)),
            pl.BlockSpec((tm, half), lambda i, j: (i, cb2)),
            pl.BlockSpec((half, tn), lambda i, j: (0, j)),
            pl.BlockSpec((half, tn), lambda i, j: (1, j)),
            pl.BlockSpec((tm, tn), lambda i, j: (i, j)),
            pl.BlockSpec((1, 1, tn), lambda i, j: (i * tm // seq, 0, j)),
        ],
        out_specs=pl.BlockSpec((tm, tn), lambda i, j: (i, j)),
        out_shape=jax.ShapeDtypeStruct((n, d), F32),
        compiler_params=_cparams(("arbitrary", "arbitrary")),
        name="outproj",
    )(o1, o2, w, w, x2, gate.reshape(bsz, 1, d))


def _route_kernel(x_ref, g_ref, sh_ref, sc_ref, wr_ref, rb_ref,
                  h_ref, e8_ref, pos8_ref, gate8_ref, cnt_ref,
                  carry, tri_tok, tri_exp):
    i = pl.program_id(0)
    tm = ROUTE_TM
    per_group = N_EXPERTS // N_GROUPS

    @pl.when(i == 0)
    def _():
        carry[...] = jnp.zeros(carry.shape, F32)
        r = lax.broadcasted_iota(I32, (tm, tm), 0)
        c = lax.broadcasted_iota(I32, (tm, tm), 1)
        tri_tok[...] = jnp.where(r <= c, 1.0, 0.0).astype(BF16)
        re = lax.broadcasted_iota(I32, (N_EXPERTS, N_EXPERTS), 0)
        ce = lax.broadcasted_iota(I32, (N_EXPERTS, N_EXPERTS), 1)
        tri_exp[...] = jnp.where(ce < re, 1.0, 0.0).astype(BF16)

    h = _norm_modulate(x_ref[...], g_ref[...], sh_ref[0], sc_ref[0])
    h_ref[...] = h
    logits = lax.dot_general(wr_ref[...], h, (((1,), (1,)), ((), ())), precision=HIGHEST,
                             preferred_element_type=F32)
    scores = jax.nn.sigmoid(logits)
    sel = scores + rb_ref[...]

    sel3 = sel.reshape(N_GROUPS, per_group, tm)
    within = lax.broadcasted_iota(I32, sel3.shape, 1)
    max1 = jnp.max(sel3, axis=1, keepdims=True)
    first = jnp.min(jnp.where(sel3 == max1, within, per_group), axis=1, keepdims=True)
    max2 = jnp.max(jnp.where(within == first, -jnp.inf, sel3), axis=1, keepdims=True)
    gscore = max1 + max2

    gidx = lax.broadcasted_iota(I32, gscore.shape, 0)
    grank = jnp.zeros(gscore.shape, I32)
    for g in range(N_GROUPS):
        other = gscore[g:g + 1]
        beats = (other > gscore) | ((other == gscore) & (gidx > g))
        grank = grank + jnp.where(beats, 1, 0)
    keep = grank < TOPK_GROUPS
    masked = jnp.where(keep, sel3, NEG_INF).reshape(N_EXPERTS, tm)

    eidx = lax.broadcasted_iota(I32, (N_EXPERTS, tm), 0)
    erank = jnp.zeros((N_EXPERTS, tm), I32)
    for e in range(N_EXPERTS):
        other = masked[e:e + 1, :]
        beats = (other > masked) | ((other == masked) & (eidx > e))
        erank = erank + jnp.where(beats, 1, 0)
    chosen = erank < TOP_K

    gates = jnp.where(chosen, scores, 0.0)
    gates = gates / jnp.sum(gates, axis=0, keepdims=True) * ROUTED_SCALE

    chosen_b = jnp.where(chosen, 1.0, 0.0).astype(BF16)
    slot = jnp.dot(tri_exp[...], chosen_b, preferred_element_type=F32)
    csum = jnp.dot(chosen_b, tri_tok[...], preferred_element_type=F32)
    pos = carry[:, 0:1] + csum - 1.0
    eidx_f = eidx.astype(F32)
    e_rows, p_rows, g_rows = [], [], []
    for kk in range(TOP_K):
        pick = chosen & (slot == float(kk))
        e_rows.append(jnp.sum(jnp.where(pick, eidx_f, 0.0), axis=0, keepdims=True))
        p_rows.append(jnp.sum(jnp.where(pick, pos, 0.0), axis=0, keepdims=True))
        g_rows.append(jnp.sum(jnp.where(pick, gates, 0.0), axis=0, keepdims=True))
    e8_ref[...] = jnp.concatenate(e_rows, axis=0).astype(I32)
    pos8_ref[...] = jnp.concatenate(p_rows, axis=0).astype(I32)
    gate8_ref[...] = jnp.concatenate(g_rows, axis=0)
    total = carry[...] + csum[:, tm - 1:tm]
    carry[...] = total
    cnt_ref[...] = total


def _route(x2, seq, gain, shift, scale, w_router, router_bias):
    n, d = x2.shape
    tm = ROUTE_TM
    bsz = shift.shape[0]
    return pl.pallas_call(
        _route_kernel,
        grid=(n // tm,),
        in_specs=[
            pl.BlockSpec((tm, d), lambda i: (i, 0)),
            pl.BlockSpec((1, d), lambda i: (0, 0)),
            pl.BlockSpec((1, 1, d), lambda i: (i * tm // seq, 0, 0)),
            pl.BlockSpec((1, 1, d), lambda i: (i * tm // seq, 0, 0)),
            pl.BlockSpec((N_EXPERTS, d), lambda i: (0, 0)),
            pl.BlockSpec((N_EXPERTS, 1), lambda i: (0, 0)),
        ],
        out_specs=[
            pl.BlockSpec((tm, d), lambda i: (i, 0)),
            pl.BlockSpec((TOP_K, tm), lambda i: (0, i)),
            pl.BlockSpec((TOP_K, tm), lambda i: (0, i)),
            pl.BlockSpec((TOP_K, tm), lambda i: (0, i)),
            pl.BlockSpec((N_EXPERTS, LANES), lambda i: (0, 0)),
        ],
        out_shape=[
            jax.ShapeDtypeStruct((n, d), F32),
            jax.ShapeDtypeStruct((TOP_K, n), I32),
            jax.ShapeDtypeStruct((TOP_K, n), I32),
            jax.ShapeDtypeStruct((TOP_K, n), F32),
            jax.ShapeDtypeStruct((N_EXPERTS, LANES), F32),
        ],
        scratch_shapes=[
            pltpu.VMEM((N_EXPERTS, LANES), F32),
            pltpu.VMEM((tm, tm), BF16),
            pltpu.VMEM((N_EXPERTS, N_EXPERTS), BF16),
        ],
        compiler_params=_cparams(("arbitrary",)),
        name="route",
    )(x2, gain.reshape(1, d), shift.reshape(bsz, 1, d), scale.reshape(bsz, 1, d),
      w_router.T, router_bias.reshape(N_EXPERTS, 1))


def _tables_kernel(cnt_ref, e8_ref, pos8_ref, dest_ref, bexp_ref, nused_ref, *, nb_pad):
    counts = cnt_ref[...]
    blocks = jnp.ceil(counts / float(MOE_BM))
    re = lax.broadcasted_iota(I32, (N_EXPERTS, N_EXPERTS), 0)
    ce = lax.broadcasted_iota(I32, (N_EXPERTS, N_EXPERTS), 1)
    tri = jnp.where(ce < re, 1.0, 0.0).astype(BF16)
    start_blk = jnp.dot(tri, blocks.astype(BF16), preferred_element_type=F32)
    end_blk = start_blk + blocks
    e8 = e8_ref[...]
    dest = pos8_ref[...].astype(F32)
    for e in range(N_EXPERTS):
        dest = dest + jnp.where(e8 == e, start_blk[e:e + 1, 0:1] * float(MOE_BM), 0.0)
    dest_ref[...] = dest.astype(I32)
    bidx = lax.broadcasted_iota(I32, (N_EXPERTS, nb_pad), 1).astype(F32)
    owner = jnp.sum(jnp.where(end_blk[:, 0:1] <= bidx, 1.0, 0.0), axis=0, keepdims=True)
    bexp_ref[...] = jnp.minimum(owner, float(N_EXPERTS - 1)).astype(I32)
    nused_ref[...] = end_blk[N_EXPERTS - 1:N_EXPERTS, :].astype(I32)


def _tables(counts, e8, pos8, nb):
    n = e8.shape[1]
    tm = 2048
    nb_pad = -(-nb // LANES) * LANES
    kern = functools.partial(_tables_kernel, nb_pad=nb_pad)
    return pl.pallas_call(
        kern,
        grid=(n // tm,),
        in_specs=[
            pl.BlockSpec((N_EXPERTS, LANES), lambda i: (0, 0)),
            pl.BlockSpec((TOP_K, tm), lambda i: (0, i)),
            pl.BlockSpec((TOP_K, tm), lambda i: (0, i)),
        ],
        out_specs=[
            pl.BlockSpec((TOP_K, tm), lambda i: (0, i)),
            pl.BlockSpec((1, nb_pad), lambda i: (0, 0)),
            pl.BlockSpec((1, LANES), lambda i: (0, 0)),
        ],
        out_shape=[
            jax.ShapeDtypeStruct((TOP_K, n), I32),
            jax.ShapeDtypeStruct((1, nb_pad), I32),
            jax.ShapeDtypeStruct((1, LANES), I32),
        ],
        compiler_params=_cparams(("arbitrary",)),
        name="tables",
    )(counts, e8, pos8)


def _experts_kernel(bexp_ref, nused_ref, idx_hbm, h_hbm, wu_ref, wd_ref, y_ref,
                    xbuf, idx_smem, sem_idx, sem_rows):
    b = pl.program_id(0)
    nu = nused_ref[0]
    slot = b % 2
    bm = MOE_BM

    def idx_copy(blk, sl):
        return pltpu.make_async_copy(idx_hbm.at[blk], idx_smem.at[sl], sem_idx.at[sl])

    def issue_rows(sl):
        def body(r, carry):
            tok = idx_smem[sl, r]
            pltpu.make_async_copy(h_hbm.at[pl.ds(tok, 1), :], xbuf.at[sl, pl.ds(r, 1), :],
                                  sem_rows.at[sl]).start()
            return carry
        lax.fori_loop(0, bm, body, 0, unroll=8)

    @pl.when(b == 0)
    def _():
        idx_copy(0, 0).start()
        idx_copy(0, 0).wait()
        issue_rows(0)

        @pl.when(nu > 1)
        def _():
            idx_copy(1, 1).start()

    @pl.when(b + 1 < nu)
    def _():
        idx_copy(b + 1, 1 - slot).wait()
        issue_rows(1 - slot)

    @pl.when(b + 2 < nu)
    def _():
        idx_copy(b + 2, slot).start()

    @pl.when(b < nu)
    def _():
        pltpu.make_async_copy(h_hbm.at[pl.ds(0, bm), :], xbuf.at[slot], sem_rows.at[slot]).wait()
        x = xbuf[slot].astype(BF16)
        up = jnp.dot(x, wu_ref[...], preferred_element_type=F32)
        act = _silu(up[:, :EXPERT_DIM]) * up[:, EXPERT_DIM:]
        y_ref[...] = jnp.dot(act.astype(BF16), wd_ref[...], preferred_element_type=F32)

    @pl.when(b >= nu)
    def _():
        y_ref[...] = jnp.zeros(y_ref.shape, F32)


def _experts(bexp, nused, idx_blocks, h, w_up, w_down):
    nb, bm = idx_blocks.shape
    n, d = h.shape
    grid_spec = pltpu.PrefetchScalarGridSpec(
        num_scalar_prefetch=2,
        grid=(nb,),
        in_specs=[
            pl.BlockSpec(memory_space=pl.ANY),
            pl.BlockSpec(memory_space=pl.ANY),
            pl.BlockSpec((None, d, 2 * EXPERT_DIM), lambda b, be, nu: (be[b], 0, 0)),
            pl.BlockSpec((None, EXPERT_DIM, d), lambda b, be, nu: (be[b], 0, 0)),
        ],
        out_specs=pl.BlockSpec((bm, d), lambda b, be, nu: (b, 0)),
        scratch_shapes=[
            pltpu.VMEM((2, bm, d), F32),
            pltpu.SMEM((2, bm), I32),
            pltpu.SemaphoreType.DMA((2,)),
            pltpu.SemaphoreType.DMA((2,)),
        ],
    )
    return pl.pallas_call(
        _experts_kernel,
        grid_spec=grid_spec,
        out_shape=jax.ShapeDtypeStruct((nb * bm, d), F32),
        compiler_params=_cparams(("arbitrary",)),
        name="experts",
    )(bexp, nused, idx_blocks, h, w_up, w_down)


def _combine_kernel(tab_hbm, y_hbm, gate_ref, h_ref, x_ref, gf_ref, wsu_ref, wsd_ref, out_ref,
                    ybuf, tab_smem, sem_tab, sem_rows, *, n_tiles):
    i = pl.program_id(0)
    slot = i % 2
    tm = COMB_TM

    def tab_copy(t, sl):
        return pltpu.make_async_copy(tab_hbm.at[t], tab_smem.at[sl], sem_tab.at[sl])

    def issue_rows(sl):
        for kk in range(TOP_K):
            def body(r, carry, kk=kk):
                row = tab_smem[sl, kk * tm + r]
                pltpu.make_async_copy(y_hbm.at[pl.ds(row, 1), :],
                                      ybuf.at[sl, kk, pl.ds(r, 1), :],
                                      sem_rows.at[sl]).start()
                return carry
            lax.fori_loop(0, tm, body, 0, unroll=8)

    @pl.when(i == 0)
    def _():
        tab_copy(0, 0).start()
        tab_copy(0, 0).wait()
        issue_rows(0)
        if n_tiles > 1:
            tab_copy(1, 1).start()

    @pl.when(i + 1 < n_tiles)
    def _():
        tab_copy(i + 1, 1 - slot).wait()
        issue_rows(1 - slot)

    @pl.when(i + 2 < n_tiles)
    def _():
        tab_copy(i + 2, slot).start()

    hb = h_ref[...].astype(BF16)
    up = jnp.dot(hb, wsu_ref[...], preferred_element_type=F32)
    half = wsd_ref.shape[0]
    act = _silu(up[:, :half]) * up[:, half:]
    total = jnp.dot(act.astype(BF16), wsd_ref[...], preferred_element_type=F32)

    for kk in range(TOP_K):
        pltpu.make_async_copy(y_hbm.at[pl.ds(0, tm), :], ybuf.at[slot, kk],
                              sem_rows.at[slot]).wait()
    gate = gate_ref[...]
    routed = gate[:, 0:1] * ybuf[slot, 0]
    for kk in range(1, TOP_K):
        routed = routed + gate[:, kk:kk + 1] * ybuf[slot, kk]
    out_ref[...] = x_ref[...] + gf_ref[0] * (routed + total)


def _combine(tab, y, gate_t, h, x2, g_f, w_su, w_sd, seq):
    n, d = x2.shape
    tm = COMB_TM
    n_tiles = n // tm
    bsz = g_f.shape[0]
    kern = functools.partial(_combine_kernel, n_tiles=n_tiles)
    return pl.pallas_call(
        kern,
        grid=(n_tiles,),
        in_specs=[
            pl.BlockSpec(memory_space=pl.ANY),
            pl.BlockSpec(memory_space=pl.ANY),
            pl.BlockSpec((tm, TOP_K), lambda i: (i, 0)),
            pl.BlockSpec((tm, d), lambda i: (i, 0)),
            pl.BlockSpec((tm, d), lambda i: (i, 0)),
            pl.BlockSpec((1, 1, d), lambda i: (i * tm // seq, 0, 0)),
            pl.BlockSpec(w_su.shape, lambda i: (0, 0)),
            pl.BlockSpec(w_sd.shape, lambda i: (0, 0)),
        ],
        out_specs=pl.BlockSpec((tm, d), lambda i: (i, 0)),
        out_shape=jax.ShapeDtypeStruct((n, d), F32),
        scratch_shapes=[
            pltpu.VMEM((2, TOP_K, tm, d), F32),
            pltpu.SMEM((2, TOP_K * tm), I32),
            pltpu.SemaphoreType.DMA((2,)),
            pltpu.SemaphoreType.DMA((2,)),
        ],
        compiler_params=_cparams(("arbitrary",)),
        name="combine",
    )(tab, y, gate_t, h, x2, g_f.reshape(bsz, 1, d), w_su, w_sd)


def _moe(x2, seq, gain, shift, scale, g_f, w_router, router_bias, w_up, w_down, w_su, w_sd):
    n, d = x2.shape
    h, e8, pos8, gate8, counts = _route(x2, seq, gain, shift, scale, w_router, router_bias)
    nb = -(-(n * TOP_K + N_EXPERTS * (MOE_BM - 1)) // MOE_BM)
    dest8, bexp, nused = _tables(counts, e8, pos8, nb)
    tok_ids = jnp.broadcast_to(jnp.arange(n, dtype=I32)[None, :], (TOP_K, n))
    row_tok = jnp.zeros((nb * MOE_BM,), I32).at[dest8.reshape(-1)].set(tok_ids.reshape(-1))
    y = _experts(bexp[0, :nb], nused[0, :1], row_tok.reshape(nb, MOE_BM), h,
                 w_up.astype(BF16), w_down.astype(BF16))
    tab = dest8.reshape(TOP_K, n // COMB_TM, COMB_TM).transpose(1, 0, 2).reshape(
        n // COMB_TM, TOP_K * COMB_TM)
    return _combine(tab, y, gate8.T, h, x2, g_f, w_su.astype(BF16), w_sd.astype(BF16), seq)


def _alibi_slopes(nh):
    return jnp.asarray(2.0 ** (-8.0 * np.arange(1, nh + 1) / nh), dtype=F32)


def _even_weights(w_in, qk_norm):
    a_w = A_HEADS * HEAD_DIM
    b_w = B_HEADS * HEAD_DIM
    kv_w = B_KV_HEADS * HEAD_DIM
    cuts = np.cumsum([a_w, a_w, a_w, b_w, kv_w])
    qa, ka, va, qb, kb, vb = jnp.split(w_in, cuts, axis=1)

    def dup(a):
        return jnp.concatenate([a[:, hh * HEAD_DIM:(hh + 1) * HEAD_DIM]
                                for hh in range(B_KV_HEADS) for _ in range(2)], axis=1)

    w = jnp.concatenate([qa, ka, qb, dup(kb), va, dup(vb)], axis=1).astype(BF16)
    q_scale = HEAD_DIM ** -0.5
    gains = jnp.concatenate([
        jnp.tile(qk_norm[0] * q_scale, A_HEADS), jnp.tile(qk_norm[1], A_HEADS),
        jnp.tile(qk_norm[2] * q_scale, B_HEADS), jnp.tile(qk_norm[3], 2 * B_KV_HEADS),
        jnp.ones((a_w + 2 * kv_w,), F32)])
    n_norm = 2 * a_w + b_w + 2 * kv_w
    return w, gains, n_norm


def kernel(x, c, norm_mix, norm_ffn, w_ada, b_ada, ab_w_in, ab_qk_norm, ab_sinks, ab_w_out,
           c_w_in, c_qk_norm, c_lambda, c_subln, c_w_out, w_router, router_bias,
           experts_up, experts_down, shared_up, shared_down):
    bsz, seq, d = x.shape
    depth = w_ada.shape[0]
    n = bsz * seq
    ada = _ada(c, w_ada, b_ada)
    x2 = x.reshape(n, d)
    for i in range(depth):
        sh_m, sc_m, g_m, sh_f, sc_f, g_f = jnp.split(ada[i], 6, axis=-1)
        j = i // 2
        if i % 2 == 0:
            w, gains, n_norm = _even_weights(ab_w_in[j], ab_qk_norm[j])
            qkv = _proj(x2, seq, norm_mix[i], sh_m, sc_m, w, gains, n_norm)
            pa = A_HEADS // 2
            pb = B_HEADS // 2
            q_a, k_a, q_b, k_b = 0, pa, 2 * pa, 2 * pa + pb
            v_a = k_b + B_KV_HEADS
            v_b = v_a + pa
            o_a = _attn_a(qkv, bsz, seq, q_a, k_a, v_a, _alibi_slopes(A_HEADS))
            o_b = _attn_b(qkv, bsz, seq, q_b, k_b, v_b, _alibi_slopes(B_HEADS),
                          ab_sinks[j].astype(F32))
            x2 = _outproj(o_a.reshape(n, -1), 0, o_b.reshape(n, -1), 0,
                          ab_w_out[j].astype(BF16), x2, g_m, seq)
        else:
            q_scale = HEAD_DIM ** -0.5 * LOG2E
            gains = jnp.concatenate([
                jnp.tile(c_qk_norm[j, 0] * q_scale, 2 * C_HEADS),
                jnp.tile(c_qk_norm[j, 1], 2 * C_HEADS),
                jnp.ones((2 * C_HEADS * HEAD_DIM,), F32)])
            qkv = _proj(x2, seq, norm_mix[i], sh_m, sc_m, c_w_in[j].astype(BF16), gains,
                        4 * C_HEADS * HEAD_DIM)
            lam_init = 0.8 - 0.6 * math.exp(-0.3 * i)
            o_c = _attn_c(qkv, bsz, seq, _alibi_slopes(C_HEADS), c_qk_norm[j],
                          c_lambda[j].astype(F32),
                          c_subln[j].astype(F32), lam_init).reshape(n, -1)
            x2 = _outproj(o_c, 0, o_c, 1, c_w_out[j].astype(BF16), x2, g_m, seq)
        x2 = _moe(x2, seq, norm_ffn[i], sh_f, sc_f, g_f, w_router[i], router_bias[i],
                  experts_up[i], experts_down[i], shared_up[i], shared_down[i])
    return x2.reshape(bsz, seq, d)
```

```python
import functools
import math

import jax
import jax.numpy as jnp
import numpy as np
from jax import lax
from jax.experimental import pallas as pl
from jax.experimental.pallas import tpu as pltpu

F32 = jnp.float32
BF16 = jnp.bfloat16
I32 = jnp.int32
HIGHEST = lax.Precision.HIGHEST

HEAD_DIM = 64
A_HEADS = 16
DILATIONS = (1, 4, 16)
A_MAX_DIST = 128
B_HEADS = 16
B_KV_HEADS = 2
B_MAX_DIST = 127
C_HEADS = 16
N_EXPERTS = 64
N_GROUPS = 8
TOPK_GROUPS = 4
TOP_K = 8
EXPERT_DIM = 512
ROUTED_SCALE = 2.5
RMS_EPS = 1e-6
NEG_INF = -1e30
LOG2E = math.log2(math.e)
SCORE_BOUND_MARGIN = 1.02
MAX_BOUNDED_EXPONENT = 100.0

LANES = 128
VMEM_LIMIT = 56 * 1024 * 1024

PROJ_TM = 512
PROJ_TN = 512
ATT_TILE = 2048
BAND = 128
DIFF_TQ = 512
DIFF_TK = 512
ROUTE_TM = 512
MOE_BM = 512
COMB_TM = 256


def _cparams(sem):
    return pltpu.CompilerParams(dimension_semantics=sem, vmem_limit_bytes=VMEM_LIMIT)


def _silu(x):
    return x * jax.nn.sigmoid(x)


def _ada_kernel(c_ref, w_ref, b_ref, o_ref):
    cond = _silu(c_ref[...])
    o_ref[...] = jnp.dot(cond, w_ref[...], precision=HIGHEST,
                         preferred_element_type=F32) + b_ref[...]


def _ada(c, w_ada, b_ada):
    depth, d, d6 = w_ada.shape
    b = c.shape[0]
    bp = 8
    cp = jnp.zeros((bp, d), F32).at[:b].set(c)
    tn = 1024
    out = pl.pallas_call(
        _ada_kernel,
        grid=(depth, d6 // tn),
        in_specs=[
            pl.BlockSpec((bp, d), lambda i, j: (0, 0)),
            pl.BlockSpec((None, d, tn), lambda i, j: (i, 0, j)),
            pl.BlockSpec((None, 1, tn), lambda i, j: (i, 0, j)),
        ],
        out_specs=pl.BlockSpec((None, bp, tn), lambda i, j: (i, 0, j)),
        out_shape=jax.ShapeDtypeStruct((depth, bp, d6), F32),
        compiler_params=_cparams(("arbitrary", "arbitrary")),
        name="ada",
    )(cp, w_ada, b_ada.reshape(depth, 1, d6))
    return out[:, :b]


def _norm_modulate(x, g, sh, sc):
    ms = jnp.mean(x * x, axis=-1, keepdims=True)
    y = x * lax.rsqrt(ms + RMS_EPS) * g
    return y * (1.0 + sc) + sh


def _proj_kernel(x_ref, g_ref, sh_ref, sc_ref, w_ref, hg_ref, pm_ref, o_ref, h_scr, *,
                 n_norm_cols, tn):
    j = pl.program_id(1)

    @pl.when(j == 0)
    def _():
        h = _norm_modulate(x_ref[...], g_ref[...], sh_ref[0], sc_ref[0])
        h_scr[...] = h.astype(BF16)

    acc = jnp.dot(h_scr[...], w_ref[...], preferred_element_type=F32)
    for cb in range(tn // LANES):
        a = acc[:, cb * LANES:(cb + 1) * LANES]
        is_norm = j * tn + cb * LANES < n_norm_cols

        @pl.when(is_norm)
        def _():
            sq = a * a
            hi = sq.astype(BF16)
            lo = (sq - hi.astype(F32)).astype(BF16)
            ms = (jnp.dot(hi, pm_ref[...], preferred_element_type=F32)
                  + jnp.dot(lo, pm_ref[...], preferred_element_type=F32))
            y = a * lax.rsqrt(ms + RMS_EPS) * hg_ref[:, cb * LANES:(cb + 1) * LANES]
            o_ref[cb] = y.astype(BF16)

        @pl.when(jnp.logical_not(is_norm))
        def _():
            o_ref[cb] = a.astype(BF16)


def _head_mean_matrix():
    r = np.arange(LANES)
    pm = (r[:, None] // HEAD_DIM == r[None, :] // HEAD_DIM).astype(np.float32) / HEAD_DIM
    return jnp.asarray(pm, BF16)


def _proj(x2, seq, gain, shift, scale, w, head_gain, n_norm_cols):
    n, d = x2.shape
    cols = w.shape[1]
    tm, tn = PROJ_TM, PROJ_TN
    bsz = shift.shape[0]
    kern = functools.partial(_proj_kernel, n_norm_cols=n_norm_cols, tn=tn)
    return pl.pallas_call(
        kern,
        grid=(n // tm, cols // tn),
        in_specs=[
            pl.BlockSpec((tm, d), lambda i, j: (i, 0)),
            pl.BlockSpec((1, d), lambda i, j: (0, 0)),
            pl.BlockSpec((1, 1, d), lambda i, j: (i * tm // seq, 0, 0)),
            pl.BlockSpec((1, 1, d), lambda i, j: (i * tm // seq, 0, 0)),
            pl.BlockSpec((d, tn), lambda i, j: (0, j)),
            pl.BlockSpec((1, tn), lambda i, j: (0, j)),
            pl.BlockSpec((LANES, LANES), lambda i, j: (0, 0)),
        ],
        out_specs=pl.BlockSpec((tn // LANES, tm, LANES), lambda i, j: (j, i, 0)),
        out_shape=jax.ShapeDtypeStruct((cols // LANES, n, LANES), BF16),
        scratch_shapes=[pltpu.VMEM((tm, d), BF16)],
        compiler_params=_cparams(("arbitrary", "arbitrary")),
        name="proj",
    )(x2, gain.reshape(1, d), shift.reshape(bsz, 1, d), scale.reshape(bsz, 1, d), w,
      head_gain.reshape(1, cols), _head_mean_matrix())


def _band_block(qz, kk, v_aug, bias, sink=None):
    s = lax.dot_general(qz, kk, (((1,), (1,)), ((), ())), preferred_element_type=F32) + bias
    m = jnp.max(s, axis=-1, keepdims=True)
    if sink is not None:
        m = jnp.maximum(m, sink)
    p = jnp.exp(s - m)
    res = jnp.dot(p.astype(BF16), v_aug, preferred_element_type=F32)
    return res, m


def _band_masks(max_dist):
    qi = lax.broadcasted_iota(I32, (BAND, 2 * BAND), 0)
    kj = lax.broadcasted_iota(I32, (BAND, 2 * BAND), 1)
    dist = BAND + qi - kj
    valid = (dist >= 0) & (dist <= max_dist)
    valid_first = valid & (kj >= BAND)
    return dist.astype(F32), valid, valid_first


def _attn_a_kernel(slopes_ref,
                   q1, q4, q16,
                   k1c, k1p, k4c, k4p, k16c, k16p,
                   v1c, v1p, v4c, v4p, v16c, v16p,
                   o_ref, kcat, vcat, st):
    p = pl.program_id(1)
    n = pl.program_id(2)
    lane = lax.broadcasted_iota(I32, (BAND, LANES), 1)
    distf, valid, valid_first = _band_masks(A_MAX_DIST)
    seq_start = n == 0
    branches = ((1, q1, k1c, k1p, v1c, v1p), (4, q4, k4c, k4p, v4c, v4p),
                (16, q16, k16c, k16p, v16c, v16p))
    for bi, (d, qv, kc, kp, vc, vp) in enumerate(branches):
        n_l = ATT_TILE // BAND // d
        rows = n_l * BAND
        biases = []
        for h in range(2):
            slope = slopes_ref[2 * p + h] * float(d)
            b_full = jnp.where(valid, -slope * distf, NEG_INF)
            b_first = jnp.where(valid_first, -slope * distf, NEG_INF)
            biases.append((b_full, jnp.where(seq_start, b_first, b_full)))
        for r in range(d):
            cs = slice(r * LANES, (r + 1) * LANES)
            kcat[0:BAND, :] = kp[:, cs]
            kcat[BAND:BAND + rows, :] = kc[:, cs]
            vprev = vp[:, cs]
            vcur = vc[:, cs]
            for h in range(2):
                vcat[h, 0:BAND, :] = jnp.where(lane // HEAD_DIM == h, vprev, 1.0).astype(BF16)
                lane_k = lax.broadcasted_iota(I32, (rows, LANES), 1)
                vcat[h, BAND:BAND + rows, :] = jnp.where(
                    lane_k // HEAD_DIM == h, vcur, 1.0).astype(BF16)
            for h in range(2):
                m_lane = LANES - 1 if h == 0 else 0
                for jb in range(n_l):
                    qb = qv[jb * BAND:(jb + 1) * BAND, cs]
                    qz = jnp.where(lane // HEAD_DIM == h, qb, 0.0).astype(BF16)
                    kk = kcat[jb * BAND:jb * BAND + 2 * BAND, :]
                    vv = vcat[h, jb * BAND:jb * BAND + 2 * BAND, :]
                    bias = biases[h][1] if jb == 0 else biases[h][0]
                    res, m = _band_block(qz, kk, vv, bias)
                    res = jnp.where(lane == m_lane, m, res)
                    if d == 1:
                        st[bi, h, jb * BAND:(jb + 1) * BAND, :] = res
                    else:
                        st[bi, h, pl.ds(jb * BAND * d + r, BAND, stride=d), :] = res
    for ib in range(ATT_TILE // BAND):
        rs = slice(ib * BAND, (ib + 1) * BAND)
        for h in range(2):
            m_lane = LANES - 1 if h == 0 else 0
            l_lane = HEAD_DIM if h == 0 else 1
            hs = slice(h * HEAD_DIM, (h + 1) * HEAD_DIM)
            parts = [st[bi, h, rs, :] for bi in range(3)]
            ms = [x[:, m_lane:m_lane + 1] for x in parts]
            m_all = jnp.maximum(jnp.maximum(ms[0], ms[1]), ms[2])
            num = sum(jnp.exp(mm - m_all) * x for mm, x in zip(ms, parts))
            out = num / num[:, l_lane:l_lane + 1]
            o_ref[rs, hs] = out[:, hs].astype(BF16)


def _attn_a(qkv, bsz, seq, q_cb, k_cb, v_cb, slopes):
    cb = qkv.shape[0]
    views = {d: qkv.reshape(cb, bsz, seq // d, d * LANES) for d in DILATIONS}
    n_pairs = A_HEADS // 2
    tile = ATT_TILE
    in_specs = [pl.BlockSpec(memory_space=pltpu.SMEM)]
    args = [slopes]
    for d in DILATIONS:
        in_specs.append(pl.BlockSpec((None, None, tile // d, d * LANES),
                                     lambda b, p, n: (q_cb + p, b, n, 0)))
        args.append(views[d])
    for base in (k_cb, v_cb):
        for d in DILATIONS:
            per = tile // d // BAND
            in_specs.append(pl.BlockSpec((None, None, tile // d, d * LANES),
                                         lambda b, p, n, base=base: (base + p, b, n, 0)))
            in_specs.append(pl.BlockSpec(
                (None, None, BAND, d * LANES),
                lambda b, p, n, base=base, per=per: (base + p, b, jnp.maximum(n * per - 1, 0), 0)))
            args += [views[d], views[d]]
    return pl.pallas_call(
        _attn_a_kernel,
        grid=(bsz, n_pairs, seq // tile),
        in_specs=in_specs,
        out_specs=pl.BlockSpec((None, tile, LANES), lambda b, p, n: (b, n, p)),
        out_shape=jax.ShapeDtypeStruct((bsz, seq, A_HEADS * HEAD_DIM), BF16),
        scratch_shapes=[
            pltpu.VMEM((BAND + tile, LANES), BF16),
            pltpu.VMEM((2, BAND + tile, LANES), BF16),
            pltpu.VMEM((3, 2, tile, LANES), F32),
        ],
        compiler_params=_cparams(("arbitrary", "arbitrary", "arbitrary")),
        name="attn_dilated",
    )(*args)


def _attn_b_kernel(slopes_ref, sinks_ref, q_ref, kc, kp, vc, vp, o_ref, kcat, vcat):
    p = pl.program_id(1)
    n = pl.program_id(2)
    lane = lax.broadcasted_iota(I32, (BAND, LANES), 1)
    lane_k = lax.broadcasted_iota(I32, (ATT_TILE, LANES), 1)
    distf, valid, valid_first = _band_masks(B_MAX_DIST)
    seq_start = n == 0
    n_l = ATT_TILE // BAND
    kcat[0:BAND, :] = kp[...]
    kcat[BAND:, :] = kc[...]
    vprev = vp[...]
    vcur = vc[...]
    for h in range(2):
        vcat[h, 0:BAND, :] = jnp.where(lane // HEAD_DIM == h, vprev, 1.0).astype(BF16)
        vcat[h, BAND:, :] = jnp.where(lane_k // HEAD_DIM == h, vcur, 1.0).astype(BF16)
    for h in range(2):
        slope = slopes_ref[2 * p + h]
        sink = sinks_ref[2 * p + h]
        b_full = jnp.where(valid, -slope * distf, NEG_INF)
        b_first = jnp.where(seq_start, jnp.where(valid_first, -slope * distf, NEG_INF), b_full)
        l_lane = HEAD_DIM if h == 0 else 0
        for jb in range(n_l):
            qb = q_ref[jb * BAND:(jb + 1) * BAND, :]
            qz = jnp.where(lane // HEAD_DIM == h, qb, 0.0).astype(BF16)
            kk = kcat[jb * BAND:jb * BAND + 2 * BAND, :]
            vv = vcat[h, jb * BAND:jb * BAND + 2 * BAND, :]
            res, m = _band_block(qz, kk, vv, b_first if jb == 0 else b_full, sink=sink)
            denom = res[:, l_lane:l_lane + 1] + jnp.exp(sink - m)
            hs = slice(h * HEAD_DIM, (h + 1) * HEAD_DIM)
            o_ref[jb * BAND:(jb + 1) * BAND, hs] = (res / denom)[:, hs].astype(BF16)


def _attn_b(qkv, bsz, seq, q_cb, k_cb, v_cb, slopes, sinks):
    cb = qkv.shape[0]
    view = qkv.reshape(cb, bsz, seq, LANES)
    n_pairs = B_HEADS // 2
    pairs_per_kv = n_pairs // B_KV_HEADS
    tile = ATT_TILE
    per = tile // BAND

    def cur(base):
        return pl.BlockSpec((None, None, tile, LANES),
                            lambda b, p, n: (base + p // pairs_per_kv, b, n, 0))

    def prev(base):
        return pl.BlockSpec(
            (None, None, BAND, LANES),
            lambda b, p, n: (base + p // pairs_per_kv, b, jnp.maximum(n * per - 1, 0), 0))

    return pl.pallas_call(
        _attn_b_kernel,
        grid=(bsz, n_pairs, seq // tile),
        in_specs=[
            pl.BlockSpec(memory_space=pltpu.SMEM),
            pl.BlockSpec(memory_space=pltpu.SMEM),
            pl.BlockSpec((None, None, tile, LANES), lambda b, p, n: (q_cb + p, b, n, 0)),
            cur(k_cb), prev(k_cb), cur(v_cb), prev(v_cb),
        ],
        out_specs=pl.BlockSpec((None, tile, LANES), lambda b, p, n: (b, n, p)),
        out_shape=jax.ShapeDtypeStruct((bsz, seq, B_HEADS * HEAD_DIM), BF16),
        scratch_shapes=[
            pltpu.VMEM((BAND + tile, LANES), BF16),
            pltpu.VMEM((2, BAND + tile, LANES), BF16),
        ],
        compiler_params=_cparams(("arbitrary", "arbitrary", "arbitrary")),
        name="attn_swa",
    )(slopes, sinks, view, view, view, view, view)


def _attn_c_kernel(qi_tab, ki_tab, bounded_ref, consts_ref, lam_ref, subln_ref,
                   q_ref, k_ref, v_ref, o_ref, m_scr, l_scr, acc_scr, bias_scr, *, lam_init):
    h = pl.program_id(1)
    step = pl.program_id(2)
    qi = qi_tab[step]
    ki = ki_tab[step]
    tq, tk = DIFF_TQ, DIFF_TK
    slope2 = consts_ref[h] * LOG2E
    bound2 = consts_ref[C_HEADS]
    bounded = bounded_ref[0] == 1
    tile_off = slope2 * ((ki - qi) * tk).astype(F32)

    @pl.when(step == 0)
    def _():
        rel = (lax.broadcasted_iota(I32, (tq, tk), 1)
               - lax.broadcasted_iota(I32, (tq, tk), 0)).astype(F32)
        bias = slope2 * rel
        bias_scr[0] = bias
        bias_scr[1] = jnp.where(rel <= 0.0, bias, NEG_INF)

    @pl.when(ki == 0)
    def _():
        m_scr[...] = jnp.full(m_scr.shape, NEG_INF, F32)
        l_scr[...] = jnp.zeros(l_scr.shape, F32)
        acc_scr[...] = jnp.zeros(acc_scr.shape, F32)

    def scores(a, q, k, lane):
        qz = jnp.where(lane // HEAD_DIM == a, q, 0.0).astype(BF16)
        return lax.dot_general(qz, k, (((1,), (1,)), ((), ())), preferred_element_type=F32)

    def update_online(bias_idx):
        q = q_ref[...]
        k = k_ref[...]
        v = v_ref[...]
        lane = lax.broadcasted_iota(I32, (tq, LANES), 1)
        for a in range(2):
            s = scores(a, q, k, lane) + bias_scr[bias_idx]
            m_prev = m_scr[a]
            m_new = jnp.maximum(m_prev, jnp.max(s, axis=-1, keepdims=True) + tile_off)
            alpha = jnp.exp2(m_prev - m_new)
            pr = jnp.exp2(s - (m_new - tile_off))
            l_scr[a] = alpha * l_scr[a] + jnp.sum(pr, axis=-1, keepdims=True)
            acc_scr[a, :, 0:LANES] = alpha * acc_scr[a, :, 0:LANES] + jnp.dot(
                pr.astype(BF16), v, preferred_element_type=F32)
            m_scr[a] = m_new

    def update_bounded(bias_idx):
        q = q_ref[...]
        k = k_ref[...]
        v = v_ref[...]
        lane = lax.broadcasted_iota(I32, (tq, LANES), 1)
        bias = bias_scr[bias_idx] + (tile_off - bound2)
        v_aug = jnp.concatenate([v, jnp.ones_like(v)], axis=1)
        for a in range(2):
            pr = jnp.exp2(scores(a, q, k, lane) + bias).astype(BF16)
            acc_scr[a] = acc_scr[a] + jnp.dot(pr, v_aug, preferred_element_type=F32)

    def finish(o1, o2):
        lp = lam_ref[...]
        lam = (jnp.exp(jnp.sum(lp[0:1] * lp[1:2], axis=-1, keepdims=True))
               - jnp.exp(jnp.sum(lp[2:3] * lp[3:4], axis=-1, keepdims=True)) + lam_init)
        o = o1 - lam * o2
        ms = jnp.mean(o * o, axis=-1, keepdims=True)
        o = o * lax.rsqrt(ms + RMS_EPS) * subln_ref[...] * (1.0 - lam_init)
        o_ref[...] = o.astype(BF16)

    @pl.when(bounded & (ki < qi))
    def _():
        update_bounded(0)

    @pl.when(bounded & (ki == qi))
    def _():
        update_bounded(1)
        finish(acc_scr[0, :, 0:LANES] / acc_scr[0, :, LANES:],
               acc_scr[1, :, 0:LANES] / acc_scr[1, :, LANES:])

    @pl.when(jnp.logical_not(bounded) & (ki < qi))
    def _():
        update_online(0)

    @pl.when(jnp.logical_not(bounded) & (ki == qi))
    def _():
        update_online(1)
        finish(acc_scr[0, :, 0:LANES] / l_scr[0], acc_scr[1, :, 0:LANES] / l_scr[1])


def _attn_c(qkv, bsz, seq, slopes, qk_gains, lam_params, subln, lam_init):
    cb = qkv.shape[0]
    view = qkv.reshape(cb, bsz, seq, LANES)
    tq, tk = DIFF_TQ, DIFF_TK
    assert tq == tk
    nq = seq // tq
    pairs = [(i, j) for i in range(nq) for j in range(i + 1)]
    qi_tab = jnp.asarray([p[0] for p in pairs], I32)
    ki_tab = jnp.asarray([p[1] for p in pairs], I32)
    g_q = jnp.max(jnp.abs(qk_gains[0].astype(F32)))
    g_k = jnp.max(jnp.abs(qk_gains[1].astype(F32)))
    bound2 = SCORE_BOUND_MARGIN * HEAD_DIM ** 0.5 * g_q * g_k * LOG2E
    bounded = (2.0 * bound2 < MAX_BOUNDED_EXPONENT).astype(I32).reshape(1)
    consts = jnp.concatenate([slopes, bound2.reshape(1)])
    kern = functools.partial(_attn_c_kernel, lam_init=lam_init)

    def im(fn):
        return lambda b, h, s, qt, kt, bd: fn(b, h, s, qt, kt)

    grid_spec = pltpu.PrefetchScalarGridSpec(
        num_scalar_prefetch=3,
        grid=(bsz, C_HEADS, len(pairs)),
        in_specs=[
            pl.BlockSpec(memory_space=pltpu.SMEM),
            pl.BlockSpec((4, HEAD_DIM), im(lambda b, h, s, qt, kt: (0, 0))),
            pl.BlockSpec((1, 2 * HEAD_DIM), im(lambda b, h, s, qt, kt: (0, 0))),
            pl.BlockSpec((None, None, tq, LANES), im(lambda b, h, s, qt, kt: (h, b, qt[s], 0))),
            pl.BlockSpec((None, None, tk, LANES),
                         im(lambda b, h, s, qt, kt: (C_HEADS + h, b, kt[s], 0))),
            pl.BlockSpec((None, None, tk, LANES),
                         im(lambda b, h, s, qt, kt: (2 * C_HEADS + h, b, kt[s], 0))),
        ],
        out_specs=pl.BlockSpec((None, tq, LANES), im(lambda b, h, s, qt, kt: (b, qt[s], h))),
        scratch_shapes=[
            pltpu.VMEM((2, tq, 1), F32),
            pltpu.VMEM((2, tq, 1), F32),
            pltpu.VMEM((2, tq, 2 * LANES), F32),
            pltpu.VMEM((2, tq, tk), F32),
        ],
    )
    return pl.pallas_call(
        kern,
        grid_spec=grid_spec,
        out_shape=jax.ShapeDtypeStruct((bsz, seq, C_HEADS * 2 * HEAD_DIM), BF16),
        compiler_params=_cparams(("arbitrary", "arbitrary", "arbitrary")),
        name="attn_diff",
    )(qi_tab, ki_tab, bounded, consts, lam_params, subln.reshape(1, 2 * HEAD_DIM),
      view, view, view)


def _outproj_kernel(o1_ref, o2_ref, w1_ref, w2_ref, x_ref, g_ref, out_ref):
    acc = (jnp.dot(o1_ref[...], w1_ref[...], preferred_element_type=F32)
           + jnp.dot(o2_ref[...], w2_ref[...], preferred_element_type=F32))
    out_ref[...] = x_ref[...] + g_ref[0] * acc


def _outproj(o1, cb1, o2, cb2, w, x2, gate, seq):
    n, d = x2.shape
    half = w.shape[0] // 2
    tm, tn = 512, 512
    bsz = gate.shape[0]
    return pl.pallas_call(
        _outproj_kernel,
        grid=(n // tm, d // tn),
        in_specs=[
            pl.BlockSpec((tm, half), lambda i, j: (i, cb1)),
            pl.BlockSpec((tm, half), lambda i, j: (i, cb2)),
            pl.BlockSpec((half, tn), lambda i, j: (0, j)),
            pl.BlockSpec((half, tn), lambda i, j: (1, j)),
            pl.BlockSpec((tm, tn), lambda i, j: (i, j)),
            pl.BlockSpec((1, 1, tn), lambda i, j: (i * tm // seq, 0, j)),
        ],
        out_specs=pl.BlockSpec((tm, tn), lambda i, j: (i, j)),
        out_shape=jax.ShapeDtypeStruct((n, d), F32),
        compiler_params=_cparams(("arbitrary", "arbitrary")),
        name="outproj",
    )(o1, o2, w, w, x2, gate.reshape(bsz, 1, d))


PACK_SUB = 8
U32 = jnp.uint32
HIGH_HALF = np.uint32(0xFFFF0000)


def _pack_rows(vals, dst_ref, rows):
    half = vals.shape[1] // 2
    lo = lax.bitcast_convert_type(vals[:, :half].astype(BF16).astype(F32), U32)
    hi = lax.bitcast_convert_type(vals[:, half:].astype(BF16).astype(F32), U32)
    words = (lo >> 16) | (hi & HIGH_HALF)
    for s in range(PACK_SUB):
        dst_ref[pl.ds(s, rows, stride=PACK_SUB), :] = words[:, s * LANES:(s + 1) * LANES]


def _unpack_words(words):
    lo = lax.bitcast_convert_type(words << 16, F32)
    hi = lax.bitcast_convert_type(words & HIGH_HALF, F32)
    return lo, hi


def _unpack_rows(src_ref, rows):
    los, his = [], []
    for s in range(PACK_SUB):
        lo, hi = _unpack_words(src_ref[pl.ds(s, rows, stride=PACK_SUB), :])
        los.append(lo)
        his.append(hi)
    return jnp.concatenate(los + his, axis=1)


def _route_kernel(x_ref, g_ref, sh_ref, sc_ref, wr_ref, rb_ref,
                  h_ref, e8_ref, pos8_ref, gate8_ref, cnt_ref,
                  carry, tri_tok, tri_exp):
    i = pl.program_id(0)
    tm = ROUTE_TM
    per_group = N_EXPERTS // N_GROUPS

    @pl.when(i == 0)
    def _():
        carry[...] = jnp.zeros(carry.shape, F32)
        r = lax.broadcasted_iota(I32, (tm, tm), 0)
        c = lax.broadcasted_iota(I32, (tm, tm), 1)
        tri_tok[...] = jnp.where(r <= c, 1.0, 0.0).astype(BF16)
        re = lax.broadcasted_iota(I32, (N_EXPERTS, N_EXPERTS), 0)
        ce = lax.broadcasted_iota(I32, (N_EXPERTS, N_EXPERTS), 1)
        tri_exp[...] = jnp.where(ce < re, 1.0, 0.0).astype(BF16)

    h = _norm_modulate(x_ref[...], g_ref[...], sh_ref[0], sc_ref[0])
    _pack_rows(h, h_ref, tm)
    logits = lax.dot_general(wr_ref[...], h, (((1,), (1,)), ((), ())), precision=HIGHEST,
                             preferred_element_type=F32)
    scores = jax.nn.sigmoid(logits)
    sel = scores + rb_ref[...]

    sel3 = sel.reshape(N_GROUPS, per_group, tm)
    within = lax.broadcasted_iota(I32, sel3.shape, 1)
    max1 = jnp.max(sel3, axis=1, keepdims=True)
    first = jnp.min(jnp.where(sel3 == max1, within, per_group), axis=1, keepdims=True)
    max2 = jnp.max(jnp.where(within == first, -jnp.inf, sel3), axis=1, keepdims=True)
    gscore = max1 + max2

    gidx = lax.broadcasted_iota(I32, gscore.shape, 0)
    grank = jnp.zeros(gscore.shape, I32)
    for g in range(N_GROUPS):
        other = gscore[g:g + 1]
        beats = (other > gscore) | ((other == gscore) & (gidx > g))
        grank = grank + jnp.where(beats, 1, 0)
    keep = grank < TOPK_GROUPS
    masked = jnp.where(keep, sel3, NEG_INF).reshape(N_EXPERTS, tm)

    eidx = lax.broadcasted_iota(I32, (N_EXPERTS, tm), 0)
    erank = jnp.zeros((N_EXPERTS, tm), I32)
    for e in range(N_EXPERTS):
        other = masked[e:e + 1, :]
        beats = (other > masked) | ((other == masked) & (eidx > e))
        erank = erank + jnp.where(beats, 1, 0)
    chosen = erank < TOP_K

    gates = jnp.where(chosen, scores, 0.0)
    gates = gates / jnp.sum(gates, axis=0, keepdims=True) * ROUTED_SCALE

    chosen_b = jnp.where(chosen, 1.0, 0.0).astype(BF16)
    slot = jnp.dot(tri_exp[...], chosen_b, preferred_element_type=F32)
    csum = jnp.dot(chosen_b, tri_tok[...], preferred_element_type=F32)
    pos = carry[:, 0:1] + csum - 1.0
    eidx_f = eidx.astype(F32)
    e_rows, p_rows, g_rows = [], [], []
    for kk in range(TOP_K):
        pick = chosen & (slot == float(kk))
        e_rows.append(jnp.sum(jnp.where(pick, eidx_f, 0.0), axis=0, keepdims=True))
        p_rows.append(jnp.sum(jnp.where(pick, pos, 0.0), axis=0, keepdims=True))
        g_rows.append(jnp.sum(jnp.where(pick, gates, 0.0), axis=0, keepdims=True))
    e8_ref[...] = jnp.concatenate(e_rows, axis=0).astype(I32)
    pos8_ref[...] = jnp.concatenate(p_rows, axis=0).astype(I32)
    gate8_ref[...] = jnp.concatenate(g_rows, axis=0)
    total = carry[...] + csum[:, tm - 1:tm]
    carry[...] = total
    cnt_ref[...] = total


def _route(x2, seq, gain, shift, scale, w_router, router_bias):
    n, d = x2.shape
    tm = ROUTE_TM
    bsz = shift.shape[0]
    return pl.pallas_call(
        _route_kernel,
        grid=(n // tm,),
        in_specs=[
            pl.BlockSpec((tm, d), lambda i: (i, 0)),
            pl.BlockSpec((1, d), lambda i: (0, 0)),
            pl.BlockSpec((1, 1, d), lambda i: (i * tm // seq, 0, 0)),
            pl.BlockSpec((1, 1, d), lambda i: (i * tm // seq, 0, 0)),
            pl.BlockSpec((N_EXPERTS, d), lambda i: (0, 0)),
            pl.BlockSpec((N_EXPERTS, 1), lambda i: (0, 0)),
        ],
        out_specs=[
            pl.BlockSpec((tm * PACK_SUB, LANES), lambda i: (i, 0)),
            pl.BlockSpec((TOP_K, tm), lambda i: (0, i)),
            pl.BlockSpec((TOP_K, tm), lambda i: (0, i)),
            pl.BlockSpec((TOP_K, tm), lambda i: (0, i)),
            pl.BlockSpec((N_EXPERTS, LANES), lambda i: (0, 0)),
        ],
        out_shape=[
            jax.ShapeDtypeStruct((n * PACK_SUB, LANES), U32),
            jax.ShapeDtypeStruct((TOP_K, n), I32),
            jax.ShapeDtypeStruct((TOP_K, n), I32),
            jax.ShapeDtypeStruct((TOP_K, n), F32),
            jax.ShapeDtypeStruct((N_EXPERTS, LANES), F32),
        ],
        scratch_shapes=[
            pltpu.VMEM((N_EXPERTS, LANES), F32),
            pltpu.VMEM((tm, tm), BF16),
            pltpu.VMEM((N_EXPERTS, N_EXPERTS), BF16),
        ],
        compiler_params=_cparams(("arbitrary",)),
        name="route",
    )(x2, gain.reshape(1, d), shift.reshape(bsz, 1, d), scale.reshape(bsz, 1, d),
      w_router.T, router_bias.reshape(N_EXPERTS, 1))


def _tables_kernel(cnt_ref, e8_ref, pos8_ref, dest_ref, bexp_ref, nused_ref, *, nb_pad):
    counts = cnt_ref[...]
    blocks = jnp.ceil(counts / float(MOE_BM))
    re = lax.broadcasted_iota(I32, (N_EXPERTS, N_EXPERTS), 0)
    ce = lax.broadcasted_iota(I32, (N_EXPERTS, N_EXPERTS), 1)
    tri = jnp.where(ce < re, 1.0, 0.0).astype(BF16)
    start_blk = jnp.dot(tri, blocks.astype(BF16), preferred_element_type=F32)
    end_blk = start_blk + blocks
    e8 = e8_ref[...]
    dest = pos8_ref[...].astype(F32)
    for e in range(N_EXPERTS):
        dest = dest + jnp.where(e8 == e, start_blk[e:e + 1, 0:1] * float(MOE_BM), 0.0)
    dest_ref[...] = dest.astype(I32)
    bidx = lax.broadcasted_iota(I32, (N_EXPERTS, nb_pad), 1).astype(F32)
    owner = jnp.sum(jnp.where(end_blk[:, 0:1] <= bidx, 1.0, 0.0), axis=0, keepdims=True)
    bexp_ref[...] = jnp.minimum(owner, float(N_EXPERTS - 1)).astype(I32)
    nused_ref[...] = end_blk[N_EXPERTS - 1:N_EXPERTS, :].astype(I32)


def _tables(counts, e8, pos8, nb):
    n = e8.shape[1]
    tm = 2048
    nb_pad = -(-nb // LANES) * LANES
    kern = functools.partial(_tables_kernel, nb_pad=nb_pad)
    return pl.pallas_call(
        kern,
        grid=(n // tm,),
        in_specs=[
            pl.BlockSpec((N_EXPERTS, LANES), lambda i: (0, 0)),
            pl.BlockSpec((TOP_K, tm), lambda i: (0, i)),
            pl.BlockSpec((TOP_K, tm), lambda i: (0, i)),
        ],
        out_specs=[
            pl.BlockSpec((TOP_K, tm), lambda i: (0, i)),
            pl.BlockSpec((1, nb_pad), lambda i: (0, 0)),
            pl.BlockSpec((1, LANES), lambda i: (0, 0)),
        ],
        out_shape=[
            jax.ShapeDtypeStruct((TOP_K, n), I32),
            jax.ShapeDtypeStruct((1, nb_pad), I32),
            jax.ShapeDtypeStruct((1, LANES), I32),
        ],
        compiler_params=_cparams(("arbitrary",)),
        name="tables",
    )(counts, e8, pos8)


def _experts_kernel(bexp_ref, nused_ref, idx_hbm, h_hbm, wu_ref, wd_ref, y_ref,
                    xbuf, idx_smem, wu_bf, wd_bf, sem_idx, sem_rows):
    b = pl.program_id(0)
    nu = nused_ref[0]
    slot = b % 2
    bm = MOE_BM

    def idx_copy(blk, sl):
        return pltpu.make_async_copy(idx_hbm.at[blk], idx_smem.at[sl], sem_idx.at[sl])

    def issue_rows(sl):
        def body(r, carry):
            src = pl.multiple_of(idx_smem[sl, r] * PACK_SUB, PACK_SUB)
            dst = pl.multiple_of(r * PACK_SUB, PACK_SUB)
            pltpu.make_async_copy(h_hbm.at[pl.ds(src, PACK_SUB), :],
                                  xbuf.at[sl, pl.ds(dst, PACK_SUB), :],
                                  sem_rows.at[sl]).start()
            return carry
        lax.fori_loop(0, bm, body, 0, unroll=8)

    prev_expert = bexp_ref[jnp.maximum(b - 1, 0)]

    @pl.when((b == 0) | (bexp_ref[b] != prev_expert))
    def _():
        wu_bf[...] = wu_ref[...].astype(BF16)
        wd_bf[...] = wd_ref[...].astype(BF16)

    @pl.when(b == 0)
    def _():
        idx_copy(0, 0).start()
        idx_copy(0, 0).wait()
        issue_rows(0)

        @pl.when(nu > 1)
        def _():
            idx_copy(1, 1).start()

    @pl.when(b + 1 < nu)
    def _():
        idx_copy(b + 1, 1 - slot).wait()
        issue_rows(1 - slot)

    @pl.when(b + 2 < nu)
    def _():
        idx_copy(b + 2, slot).start()

    @pl.when(b < nu)
    def _():
        pltpu.make_async_copy(h_hbm.at[pl.ds(0, bm * PACK_SUB), :], xbuf.at[slot],
                              sem_rows.at[slot]).wait()
        x = _unpack_rows(xbuf.at[slot], bm).astype(BF16)
        up = jnp.dot(x, wu_bf[...], preferred_element_type=F32)
        act = _silu(up[:, :EXPERT_DIM]) * up[:, EXPERT_DIM:]
        y = jnp.dot(act.astype(BF16), wd_bf[...], preferred_element_type=F32)
        _pack_rows(y, y_ref, bm)

    @pl.when(b >= nu)
    def _():
        y_ref[...] = jnp.zeros(y_ref.shape, U32)


def _experts(bexp, nused, idx_blocks, h_packed, w_up, w_down):
    nb, bm = idx_blocks.shape
    d = w_up.shape[1]
    grid_spec = pltpu.PrefetchScalarGridSpec(
        num_scalar_prefetch=2,
        grid=(nb,),
        in_specs=[
            pl.BlockSpec(memory_space=pl.ANY),
            pl.BlockSpec(memory_space=pl.ANY),
            pl.BlockSpec((None, d, 2 * EXPERT_DIM), lambda b, be, nu: (be[b], 0, 0)),
            pl.BlockSpec((None, EXPERT_DIM, d), lambda b, be, nu: (be[b], 0, 0)),
        ],
        out_specs=pl.BlockSpec((bm * PACK_SUB, LANES), lambda b, be, nu: (b, 0)),
        scratch_shapes=[
            pltpu.VMEM((2, bm * PACK_SUB, LANES), U32),
            pltpu.SMEM((2, bm), I32),
            pltpu.VMEM((d, 2 * EXPERT_DIM), BF16),
            pltpu.VMEM((EXPERT_DIM, d), BF16),
            pltpu.SemaphoreType.DMA((2,)),
            pltpu.SemaphoreType.DMA((2,)),
        ],
    )
    return pl.pallas_call(
        _experts_kernel,
        grid_spec=grid_spec,
        out_shape=jax.ShapeDtypeStruct((nb * bm * PACK_SUB, LANES), U32),
        compiler_params=_cparams(("arbitrary",)),
        name="experts",
    )(bexp, nused, idx_blocks, h_packed, w_up, w_down)


def _combine_kernel(tab_hbm, y_hbm, gate_ref, h_ref, x_ref, gf_ref, wsu_ref, wsd_ref, out_ref,
                    ybuf, tab_smem, shared_scr, sem_tab, sem_rows, *, n_tiles):
    i = pl.program_id(0)
    slot = i % 2
    tm = COMB_TM

    def tab_copy(t, sl):
        return pltpu.make_async_copy(tab_hbm.at[t], tab_smem.at[sl], sem_tab.at[sl])

    def issue_rows(sl):
        for kk in range(TOP_K):
            def body(r, carry, kk=kk):
                src = pl.multiple_of(tab_smem[sl, kk * tm + r] * PACK_SUB, PACK_SUB)
                dst = pl.multiple_of(r * PACK_SUB, PACK_SUB)
                pltpu.make_async_copy(y_hbm.at[pl.ds(src, PACK_SUB), :],
                                      ybuf.at[sl, kk, pl.ds(dst, PACK_SUB), :],
                                      sem_rows.at[sl]).start()
                return carry
            lax.fori_loop(0, tm, body, 0, unroll=8)

    @pl.when(i == 0)
    def _():
        tab_copy(0, 0).start()
        tab_copy(0, 0).wait()
        issue_rows(0)
        if n_tiles > 1:
            tab_copy(1, 1).start()

    @pl.when(i + 1 < n_tiles)
    def _():
        tab_copy(i + 1, 1 - slot).wait()
        issue_rows(1 - slot)

    @pl.when(i + 2 < n_tiles)
    def _():
        tab_copy(i + 2, slot).start()

    hb = _unpack_rows(h_ref, tm).astype(BF16)
    up = jnp.dot(hb, wsu_ref[...], preferred_element_type=F32)
    half = wsd_ref.shape[0]
    act = _silu(up[:, :half]) * up[:, half:]
    shared_scr[...] = jnp.dot(act.astype(BF16), wsd_ref[...], preferred_element_type=F32)

    for kk in range(TOP_K):
        pltpu.make_async_copy(y_hbm.at[pl.ds(0, tm * PACK_SUB), :], ybuf.at[slot, kk],
                              sem_rows.at[slot]).wait()
    gate = gate_ref[...]
    gates = [jnp.broadcast_to(gate[:, kk:kk + 1], (tm, LANES)) for kk in range(TOP_K)]
    d_half = PACK_SUB * LANES
    for s in range(PACK_SUB):
        acc_lo = jnp.zeros((tm, LANES), F32)
        acc_hi = jnp.zeros((tm, LANES), F32)
        for kk in range(TOP_K):
            lo, hi = _unpack_words(ybuf[slot, kk, pl.ds(s, tm, stride=PACK_SUB), :])
            acc_lo = acc_lo + gates[kk] * lo
            acc_hi = acc_hi + gates[kk] * hi
        for acc, c0 in ((acc_lo, s * LANES), (acc_hi, d_half + s * LANES)):
            cs = slice(c0, c0 + LANES)
            out_ref[:, cs] = x_ref[:, cs] + gf_ref[0, :, cs] * (acc + shared_scr[:, cs])


def _combine(tab, y, gate_t, h, x2, g_f, w_su, w_sd, seq):
    n, d = x2.shape
    tm = COMB_TM
    n_tiles = n // tm
    bsz = g_f.shape[0]
    kern = functools.partial(_combine_kernel, n_tiles=n_tiles)
    return pl.pallas_call(
        kern,
        grid=(n_tiles,),
        in_specs=[
            pl.BlockSpec(memory_space=pl.ANY),
            pl.BlockSpec(memory_space=pl.ANY),
            pl.BlockSpec((tm, TOP_K), lambda i: (i, 0)),
            pl.BlockSpec((tm * PACK_SUB, LANES), lambda i: (i, 0)),
            pl.BlockSpec((tm, d), lambda i: (i, 0)),
            pl.BlockSpec((1, 1, d), lambda i: (i * tm // seq, 0, 0)),
            pl.BlockSpec(w_su.shape, lambda i: (0, 0)),
            pl.BlockSpec(w_sd.shape, lambda i: (0, 0)),
        ],
        out_specs=pl.BlockSpec((tm, d), lambda i: (i, 0)),
        out_shape=jax.ShapeDtypeStruct((n, d), F32),
        scratch_shapes=[
            pltpu.VMEM((2, TOP_K, tm * PACK_SUB, LANES), U32),
            pltpu.SMEM((2, TOP_K * tm), I32),
            pltpu.VMEM((tm, d), F32),
            pltpu.SemaphoreType.DMA((2,)),
            pltpu.SemaphoreType.DMA((2,)),
        ],
        compiler_params=_cparams(("arbitrary",)),
        name="combine",
    )(tab, y, gate_t, h, x2, g_f.reshape(bsz, 1, d), w_su, w_sd)


def _moe(x2, seq, gain, shift, scale, g_f, w_router, router_bias, w_up, w_down, w_su, w_sd):
    n, d = x2.shape
    assert d == 2 * PACK_SUB * LANES
    h, e8, pos8, gate8, counts = _route(x2, seq, gain, shift, scale, w_router, router_bias)
    nb = -(-(n * TOP_K + N_EXPERTS * (MOE_BM - 1)) // MOE_BM)
    dest8, bexp, nused = _tables(counts, e8, pos8, nb)
    tok_ids = jnp.broadcast_to(jnp.arange(n, dtype=I32)[None, :], (TOP_K, n))
    row_tok = jnp.zeros((nb * MOE_BM,), I32).at[dest8.reshape(-1)].set(tok_ids.reshape(-1))
    y = _experts(bexp[0, :nb], nused[0, :1], row_tok.reshape(nb, MOE_BM), h, w_up, w_down)
    tab = dest8.reshape(TOP_K, n // COMB_TM, COMB_TM).transpose(1, 0, 2).reshape(
        n // COMB_TM, TOP_K * COMB_TM)
    return _combine(tab, y, gate8.T, h, x2, g_f, w_su.astype(BF16), w_sd.astype(BF16), seq)


def _alibi_slopes(nh):
    return jnp.asarray(2.0 ** (-8.0 * np.arange(1, nh + 1) / nh), dtype=F32)


def _even_weights(w_in, qk_norm):
    a_w = A_HEADS * HEAD_DIM
    b_w = B_HEADS * HEAD_DIM
    kv_w = B_KV_HEADS * HEAD_DIM
    cuts = np.cumsum([a_w, a_w, a_w, b_w, kv_w])
    qa, ka, va, qb, kb, vb = jnp.split(w_in, cuts, axis=1)

    def dup(a):
        return jnp.concatenate([a[:, hh * HEAD_DIM:(hh + 1) * HEAD_DIM]
                                for hh in range(B_KV_HEADS) for _ in range(2)], axis=1)

    w = jnp.concatenate([qa, ka, qb, dup(kb), va, dup(vb)], axis=1).astype(BF16)
    q_scale = HEAD_DIM ** -0.5
    gains = jnp.concatenate([
        jnp.tile(qk_norm[0] * q_scale, A_HEADS), jnp.tile(qk_norm[1], A_HEADS),
        jnp.tile(qk_norm[2] * q_scale, B_HEADS), jnp.tile(qk_norm[3], 2 * B_KV_HEADS),
        jnp.ones((a_w + 2 * kv_w,), F32)])
    n_norm = 2 * a_w + b_w + 2 * kv_w
    return w, gains, n_norm


def kernel(x, c, norm_mix, norm_ffn, w_ada, b_ada, ab_w_in, ab_qk_norm, ab_sinks, ab_w_out,
           c_w_in, c_qk_norm, c_lambda, c_subln, c_w_out, w_router, router_bias,
           experts_up, experts_down, shared_up, shared_down):
    bsz, seq, d = x.shape
    depth = w_ada.shape[0]
    n = bsz * seq
    ada = _ada(c, w_ada, b_ada)
    x2 = x.reshape(n, d)
    for i in range(depth):
        sh_m, sc_m, g_m, sh_f, sc_f, g_f = jnp.split(ada[i], 6, axis=-1)
        j = i // 2
        if i % 2 == 0:
            w, gains, n_norm = _even_weights(ab_w_in[j], ab_qk_norm[j])
            qkv = _proj(x2, seq, norm_mix[i], sh_m, sc_m, w, gains, n_norm)
            pa = A_HEADS // 2
            pb = B_HEADS // 2
            q_a, k_a, q_b, k_b = 0, pa, 2 * pa, 2 * pa + pb
            v_a = k_b + B_KV_HEADS
            v_b = v_a + pa
            o_a = _attn_a(qkv, bsz, seq, q_a, k_a, v_a, _alibi_slopes(A_HEADS))
            o_b = _attn_b(qkv, bsz, seq, q_b, k_b, v_b, _alibi_slopes(B_HEADS),
                          ab_sinks[j].astype(F32))
            x2 = _outproj(o_a.reshape(n, -1), 0, o_b.reshape(n, -1), 0,
                          ab_w_out[j].astype(BF16), x2, g_m, seq)
        else:
            q_scale = HEAD_DIM ** -0.5 * LOG2E
            gains = jnp.concatenate([
                jnp.tile(c_qk_norm[j, 0] * q_scale, 2 * C_HEADS),
                jnp.tile(c_qk_norm[j, 1], 2 * C_HEADS),
                jnp.ones((2 * C_HEADS * HEAD_DIM,), F32)])
            qkv = _proj(x2, seq, norm_mix[i], sh_m, sc_m, c_w_in[j].astype(BF16), gains,
                        4 * C_HEADS * HEAD_DIM)
            lam_init = 0.8 - 0.6 * math.exp(-0.3 * i)
            o_c = _attn_c(qkv, bsz, seq, _alibi_slopes(C_HEADS), c_qk_norm[j],
                          c_lambda[j].astype(F32),
                          c_subln[j].astype(F32), lam_init).reshape(n, -1)
            x2 = _outproj(o_c, 0, o_c, 1, c_w_out[j].astype(BF16), x2, g_m, seq)
        x2 = _moe(x2, seq, norm_ffn[i], sh_f, sc_f, g_f, w_router[i], router_bias[i],
                  experts_up[i], experts_down[i], shared_up[i], shared_down[i])
    return x2.reshape(bsz, seq, d)
```

```python
import functools
import math

import jax
import jax.numpy as jnp
import numpy as np
from jax import lax
from jax.experimental import pallas as pl
from jax.experimental.pallas import tpu as pltpu

F32 = jnp.float32
BF16 = jnp.bfloat16
I32 = jnp.int32
HIGHEST = lax.Precision.HIGHEST

HEAD_DIM = 64
A_HEADS = 16
DILATIONS = (1, 4, 16)
A_MAX_DIST = 128
B_HEADS = 16
B_KV_HEADS = 2
B_MAX_DIST = 127
C_HEADS = 16
N_EXPERTS = 64
N_GROUPS = 8
TOPK_GROUPS = 4
TOP_K = 8
EXPERT_DIM = 512
ROUTED_SCALE = 2.5
RMS_EPS = 1e-6
NEG_INF = -1e30
LOG2E = math.log2(math.e)
SCORE_BOUND_MARGIN = 1.02
MAX_BOUNDED_EXPONENT = 100.0

LANES = 128
VMEM_LIMIT = 56 * 1024 * 1024

PROJ_TM = 512
PROJ_TN = 512
ATT_TILE = 2048
BAND = 128
DIFF_TQ = 512
DIFF_TK = 512
ROUTE_TM = 512
MOE_BM = 512
COMB_TM = 256
DISP_TM = 512


def _cparams(sem):
    return pltpu.CompilerParams(dimension_semantics=sem, vmem_limit_bytes=VMEM_LIMIT)


def _silu(x):
    return x * jax.nn.sigmoid(x)


def _ada_kernel(c_ref, w_ref, b_ref, o_ref):
    cond = _silu(c_ref[...])
    o_ref[...] = jnp.dot(cond, w_ref[...], precision=HIGHEST,
                         preferred_element_type=F32) + b_ref[...]


def _ada(c, w_ada, b_ada):
    depth, d, d6 = w_ada.shape
    b = c.shape[0]
    bp = 8
    cp = jnp.zeros((bp, d), F32).at[:b].set(c)
    tn = 1024
    out = pl.pallas_call(
        _ada_kernel,
        grid=(depth, d6 // tn),
        in_specs=[
            pl.BlockSpec((bp, d), lambda i, j: (0, 0)),
            pl.BlockSpec((None, d, tn), lambda i, j: (i, 0, j)),
            pl.BlockSpec((None, 1, tn), lambda i, j: (i, 0, j)),
        ],
        out_specs=pl.BlockSpec((None, bp, tn), lambda i, j: (i, 0, j)),
        out_shape=jax.ShapeDtypeStruct((depth, bp, d6), F32),
        compiler_params=_cparams(("arbitrary", "arbitrary")),
        name="ada",
    )(cp, w_ada, b_ada.reshape(depth, 1, d6))
    return out[:, :b]


def _norm_modulate(x, g, sh, sc):
    ms = jnp.mean(x * x, axis=-1, keepdims=True)
    y = x * lax.rsqrt(ms + RMS_EPS) * g
    return y * (1.0 + sc) + sh


def _proj_kernel(x_ref, g_ref, sh_ref, sc_ref, w_ref, hg_ref, pm_ref, o_ref, h_scr, *,
                 n_norm_cols, tn):
    j = pl.program_id(1)

    @pl.when(j == 0)
    def _():
        h = _norm_modulate(x_ref[...], g_ref[...], sh_ref[0], sc_ref[0])
        h_scr[...] = h.astype(BF16)

    acc = jnp.dot(h_scr[...], w_ref[...], preferred_element_type=F32)
    for cb in range(tn // LANES):
        a = acc[:, cb * LANES:(cb + 1) * LANES]
        is_norm = j * tn + cb * LANES < n_norm_cols

        @pl.when(is_norm)
        def _():
            sq = a * a
            hi = sq.astype(BF16)
            lo = (sq - hi.astype(F32)).astype(BF16)
            ms = (jnp.dot(hi, pm_ref[...], preferred_element_type=F32)
                  + jnp.dot(lo, pm_ref[...], preferred_element_type=F32))
            y = a * lax.rsqrt(ms + RMS_EPS) * hg_ref[:, cb * LANES:(cb + 1) * LANES]
            o_ref[cb] = y.astype(BF16)

        @pl.when(jnp.logical_not(is_norm))
        def _():
            o_ref[cb] = a.astype(BF16)


def _head_mean_matrix():
    r = np.arange(LANES)
    pm = (r[:, None] // HEAD_DIM == r[None, :] // HEAD_DIM).astype(np.float32) / HEAD_DIM
    return jnp.asarray(pm, BF16)


def _proj(x2, seq, gain, shift, scale, w, head_gain, n_norm_cols):
    n, d = x2.shape
    cols = w.shape[1]
    tm, tn = PROJ_TM, PROJ_TN
    bsz = shift.shape[0]
    kern = functools.partial(_proj_kernel, n_norm_cols=n_norm_cols, tn=tn)
    return pl.pallas_call(
        kern,
        grid=(n // tm, cols // tn),
        in_specs=[
            pl.BlockSpec((tm, d), lambda i, j: (i, 0)),
            pl.BlockSpec((1, d), lambda i, j: (0, 0)),
            pl.BlockSpec((1, 1, d), lambda i, j: (i * tm // seq, 0, 0)),
            pl.BlockSpec((1, 1, d), lambda i, j: (i * tm // seq, 0, 0)),
            pl.BlockSpec((d, tn), lambda i, j: (0, j)),
            pl.BlockSpec((1, tn), lambda i, j: (0, j)),
            pl.BlockSpec((LANES, LANES), lambda i, j: (0, 0)),
        ],
        out_specs=pl.BlockSpec((tn // LANES, tm, LANES), lambda i, j: (j, i, 0)),
        out_shape=jax.ShapeDtypeStruct((cols // LANES, n, LANES), BF16),
        scratch_shapes=[pltpu.VMEM((tm, d), BF16)],
        compiler_params=_cparams(("arbitrary", "arbitrary")),
        name="proj",
    )(x2, gain.reshape(1, d), shift.reshape(bsz, 1, d), scale.reshape(bsz, 1, d), w,
      head_gain.reshape(1, cols), _head_mean_matrix())


def _band_block(qz, kk, v_aug, bias, sink=None):
    s = lax.dot_general(qz, kk, (((1,), (1,)), ((), ())), preferred_element_type=F32) + bias
    m = jnp.max(s, axis=-1, keepdims=True)
    if sink is not None:
        m = jnp.maximum(m, sink)
    p = jnp.exp(s - m)
    res = jnp.dot(p.astype(BF16), v_aug, preferred_element_type=F32)
    return res, m


def _band_masks(max_dist):
    qi = lax.broadcasted_iota(I32, (BAND, 2 * BAND), 0)
    kj = lax.broadcasted_iota(I32, (BAND, 2 * BAND), 1)
    dist = BAND + qi - kj
    valid = (dist >= 0) & (dist <= max_dist)
    valid_first = valid & (kj >= BAND)
    return dist.astype(F32), valid, valid_first


def _attn_a_kernel(slopes_ref,
                   q1, q4, q16,
                   k1c, k1p, k4c, k4p, k16c, k16p,
                   v1c, v1p, v4c, v4p, v16c, v16p,
                   o_ref, kcat, vcat, st):
    p = pl.program_id(1)
    n = pl.program_id(2)
    lane = lax.broadcasted_iota(I32, (BAND, LANES), 1)
    distf, valid, valid_first = _band_masks(A_MAX_DIST)
    seq_start = n == 0
    branches = ((1, q1, k1c, k1p, v1c, v1p), (4, q4, k4c, k4p, v4c, v4p),
                (16, q16, k16c, k16p, v16c, v16p))
    for bi, (d, qv, kc, kp, vc, vp) in enumerate(branches):
        n_l = ATT_TILE // BAND // d
        rows = n_l * BAND
        biases = []
        for h in range(2):
            slope = slopes_ref[2 * p + h] * float(d)
            b_full = jnp.where(valid, -slope * distf, NEG_INF)
            b_first = jnp.where(valid_first, -slope * distf, NEG_INF)
            biases.append((b_full, jnp.where(seq_start, b_first, b_full)))
        for r in range(d):
            cs = slice(r * LANES, (r + 1) * LANES)
            kcat[0:BAND, :] = kp[:, cs]
            kcat[BAND:BAND + rows, :] = kc[:, cs]
            vprev = vp[:, cs]
            vcur = vc[:, cs]
            for h in range(2):
                vcat[h, 0:BAND, :] = jnp.where(lane // HEAD_DIM == h, vprev, 1.0).astype(BF16)
                lane_k = lax.broadcasted_iota(I32, (rows, LANES), 1)
                vcat[h, BAND:BAND + rows, :] = jnp.where(
                    lane_k // HEAD_DIM == h, vcur, 1.0).astype(BF16)
            for h in range(2):
                m_lane = LANES - 1 if h == 0 else 0
                for jb in range(n_l):
                    qb = qv[jb * BAND:(jb + 1) * BAND, cs]
                    qz = jnp.where(lane // HEAD_DIM == h, qb, 0.0).astype(BF16)
                    kk = kcat[jb * BAND:jb * BAND + 2 * BAND, :]
                    vv = vcat[h, jb * BAND:jb * BAND + 2 * BAND, :]
                    bias = biases[h][1] if jb == 0 else biases[h][0]
                    res, m = _band_block(qz, kk, vv, bias)
                    res = jnp.where(lane == m_lane, m, res)
                    if d == 1:
                        st[bi, h, jb * BAND:(jb + 1) * BAND, :] = res
                    else:
                        st[bi, h, pl.ds(jb * BAND * d + r, BAND, stride=d), :] = res
    for ib in range(ATT_TILE // BAND):
        rs = slice(ib * BAND, (ib + 1) * BAND)
        for h in range(2):
            m_lane = LANES - 1 if h == 0 else 0
            l_lane = HEAD_DIM if h == 0 else 1
            hs = slice(h * HEAD_DIM, (h + 1) * HEAD_DIM)
            parts = [st[bi, h, rs, :] for bi in range(3)]
            ms = [x[:, m_lane:m_lane + 1] for x in parts]
            m_all = jnp.maximum(jnp.maximum(ms[0], ms[1]), ms[2])
            num = sum(jnp.exp(mm - m_all) * x for mm, x in zip(ms, parts))
            out = num / num[:, l_lane:l_lane + 1]
            o_ref[rs, hs] = out[:, hs].astype(BF16)


def _attn_a(qkv, bsz, seq, q_cb, k_cb, v_cb, slopes):
    cb = qkv.shape[0]
    views = {d: qkv.reshape(cb, bsz, seq // d, d * LANES) for d in DILATIONS}
    n_pairs = A_HEADS // 2
    tile = ATT_TILE
    in_specs = [pl.BlockSpec(memory_space=pltpu.SMEM)]
    args = [slopes]
    for d in DILATIONS:
        in_specs.append(pl.BlockSpec((None, None, tile // d, d * LANES),
                                     lambda b, p, n: (q_cb + p, b, n, 0)))
        args.append(views[d])
    for base in (k_cb, v_cb):
        for d in DILATIONS:
            per = tile // d // BAND
            in_specs.append(pl.BlockSpec((None, None, tile // d, d * LANES),
                                         lambda b, p, n, base=base: (base + p, b, n, 0)))
            in_specs.append(pl.BlockSpec(
                (None, None, BAND, d * LANES),
                lambda b, p, n, base=base, per=per: (base + p, b, jnp.maximum(n * per - 1, 0), 0)))
            args += [views[d], views[d]]
    return pl.pallas_call(
        _attn_a_kernel,
        grid=(bsz, n_pairs, seq // tile),
        in_specs=in_specs,
        out_specs=pl.BlockSpec((None, tile, LANES), lambda b, p, n: (b, n, p)),
        out_shape=jax.ShapeDtypeStruct((bsz, seq, A_HEADS * HEAD_DIM), BF16),
        scratch_shapes=[
            pltpu.VMEM((BAND + tile, LANES), BF16),
            pltpu.VMEM((2, BAND + tile, LANES), BF16),
            pltpu.VMEM((3, 2, tile, LANES), F32),
        ],
        compiler_params=_cparams(("arbitrary", "arbitrary", "arbitrary")),
        name="attn_dilated",
    )(*args)


def _attn_b_kernel(slopes_ref, sinks_ref, q_ref, kc, kp, vc, vp, o_ref, kcat, vcat):
    p = pl.program_id(1)
    n = pl.program_id(2)
    lane = lax.broadcasted_iota(I32, (BAND, LANES), 1)
    lane_k = lax.broadcasted_iota(I32, (ATT_TILE, LANES), 1)
    distf, valid, valid_first = _band_masks(B_MAX_DIST)
    seq_start = n == 0
    n_l = ATT_TILE // BAND
    kcat[0:BAND, :] = kp[...]
    kcat[BAND:, :] = kc[...]
    vprev = vp[...]
    vcur = vc[...]
    for h in range(2):
        vcat[h, 0:BAND, :] = jnp.where(lane // HEAD_DIM == h, vprev, 1.0).astype(BF16)
        vcat[h, BAND:, :] = jnp.where(lane_k // HEAD_DIM == h, vcur, 1.0).astype(BF16)
    for h in range(2):
        slope = slopes_ref[2 * p + h]
        sink = sinks_ref[2 * p + h]
        b_full = jnp.where(valid, -slope * distf, NEG_INF)
        b_first = jnp.where(seq_start, jnp.where(valid_first, -slope * distf, NEG_INF), b_full)
        l_lane = HEAD_DIM if h == 0 else 0
        for jb in range(n_l):
            qb = q_ref[jb * BAND:(jb + 1) * BAND, :]
            qz = jnp.where(lane // HEAD_DIM == h, qb, 0.0).astype(BF16)
            kk = kcat[jb * BAND:jb * BAND + 2 * BAND, :]
            vv = vcat[h, jb * BAND:jb * BAND + 2 * BAND, :]
            res, m = _band_block(qz, kk, vv, b_first if jb == 0 else b_full, sink=sink)
            denom = res[:, l_lane:l_lane + 1] + jnp.exp(sink - m)
            hs = slice(h * HEAD_DIM, (h + 1) * HEAD_DIM)
            o_ref[jb * BAND:(jb + 1) * BAND, hs] = (res / denom)[:, hs].astype(BF16)


def _attn_b(qkv, bsz, seq, q_cb, k_cb, v_cb, slopes, sinks):
    cb = qkv.shape[0]
    view = qkv.reshape(cb, bsz, seq, LANES)
    n_pairs = B_HEADS // 2
    pairs_per_kv = n_pairs // B_KV_HEADS
    tile = ATT_TILE
    per = tile // BAND

    def cur(base):
        return pl.BlockSpec((None, None, tile, LANES),
                            lambda b, p, n: (base + p // pairs_per_kv, b, n, 0))

    def prev(base):
        return pl.BlockSpec(
            (None, None, BAND, LANES),
            lambda b, p, n: (base + p // pairs_per_kv, b, jnp.maximum(n * per - 1, 0), 0))

    return pl.pallas_call(
        _attn_b_kernel,
        grid=(bsz, n_pairs, seq // tile),
        in_specs=[
            pl.BlockSpec(memory_space=pltpu.SMEM),
            pl.BlockSpec(memory_space=pltpu.SMEM),
            pl.BlockSpec((None, None, tile, LANES), lambda b, p, n: (q_cb + p, b, n, 0)),
            cur(k_cb), prev(k_cb), cur(v_cb), prev(v_cb),
        ],
        out_specs=pl.BlockSpec((None, tile, LANES), lambda b, p, n: (b, n, p)),
        out_shape=jax.ShapeDtypeStruct((bsz, seq, B_HEADS * HEAD_DIM), BF16),
        scratch_shapes=[
            pltpu.VMEM((BAND + tile, LANES), BF16),
            pltpu.VMEM((2, BAND + tile, LANES), BF16),
        ],
        compiler_params=_cparams(("arbitrary", "arbitrary", "arbitrary")),
        name="attn_swa",
    )(slopes, sinks, view, view, view, view, view)


def _attn_c_kernel(qi_tab, ki_tab, bounded_ref, consts_ref, lam_ref, subln_ref,
                   q_ref, k_ref, v_ref, o_ref, m_scr, l_scr, acc_scr, bias_scr, *, lam_init):
    h = pl.program_id(1)
    step = pl.program_id(2)
    qi = qi_tab[step]
    ki = ki_tab[step]
    tq, tk = DIFF_TQ, DIFF_TK
    slope2 = consts_ref[h] * LOG2E
    bound2 = consts_ref[C_HEADS]
    bounded = bounded_ref[0] == 1
    tile_off = slope2 * ((ki - qi) * tk).astype(F32)

    @pl.when(step == 0)
    def _():
        rel = (lax.broadcasted_iota(I32, (tq, tk), 1)
               - lax.broadcasted_iota(I32, (tq, tk), 0)).astype(F32)
        bias = slope2 * rel
        bias_scr[0] = bias
        bias_scr[1] = jnp.where(rel <= 0.0, bias, NEG_INF)

    @pl.when(ki == 0)
    def _():
        m_scr[...] = jnp.full(m_scr.shape, NEG_INF, F32)
        l_scr[...] = jnp.zeros(l_scr.shape, F32)
        acc_scr[...] = jnp.zeros(acc_scr.shape, F32)

    def scores(a, q, k, lane):
        qz = jnp.where(lane // HEAD_DIM == a, q, 0.0).astype(BF16)
        return lax.dot_general(qz, k, (((1,), (1,)), ((), ())), preferred_element_type=F32)

    def update_online(bias_idx):
        q = q_ref[...]
        k = k_ref[...]
        v = v_ref[...]
        lane = lax.broadcasted_iota(I32, (tq, LANES), 1)
        for a in range(2):
            s = scores(a, q, k, lane) + bias_scr[bias_idx]
            m_prev = m_scr[a]
            m_new = jnp.maximum(m_prev, jnp.max(s, axis=-1, keepdims=True) + tile_off)
            alpha = jnp.exp2(m_prev - m_new)
            pr = jnp.exp2(s - (m_new - tile_off))
            l_scr[a] = alpha * l_scr[a] + jnp.sum(pr, axis=-1, keepdims=True)
            acc_scr[a, :, 0:LANES] = alpha * acc_scr[a, :, 0:LANES] + jnp.dot(
                pr.astype(BF16), v, preferred_element_type=F32)
            m_scr[a] = m_new

    def update_bounded(bias_idx):
        q = q_ref[...]
        k = k_ref[...]
        v = v_ref[...]
        lane = lax.broadcasted_iota(I32, (tq, LANES), 1)
        bias = bias_scr[bias_idx] + (tile_off - bound2)
        v_aug = jnp.concatenate([v, jnp.ones_like(v)], axis=1)
        for a in range(2):
            pr = jnp.exp2(scores(a, q, k, lane) + bias).astype(BF16)
            acc_scr[a] = acc_scr[a] + jnp.dot(pr, v_aug, preferred_element_type=F32)

    def finish(o1, o2):
        lp = lam_ref[...]
        lam = (jnp.exp(jnp.sum(lp[0:1] * lp[1:2], axis=-1, keepdims=True))
               - jnp.exp(jnp.sum(lp[2:3] * lp[3:4], axis=-1, keepdims=True)) + lam_init)
        o = o1 - lam * o2
        ms = jnp.mean(o * o, axis=-1, keepdims=True)
        o = o * lax.rsqrt(ms + RMS_EPS) * subln_ref[...] * (1.0 - lam_init)
        o_ref[...] = o.astype(BF16)

    @pl.when(bounded & (ki < qi))
    def _():
        update_bounded(0)

    @pl.when(bounded & (ki == qi))
    def _():
        update_bounded(1)
        finish(acc_scr[0, :, 0:LANES] / acc_scr[0, :, LANES:],
               acc_scr[1, :, 0:LANES] / acc_scr[1, :, LANES:])

    @pl.when(jnp.logical_not(bounded) & (ki < qi))
    def _():
        update_online(0)

    @pl.when(jnp.logical_not(bounded) & (ki == qi))
    def _():
        update_online(1)
        finish(acc_scr[0, :, 0:LANES] / l_scr[0], acc_scr[1, :, 0:LANES] / l_scr[1])


def _attn_c(qkv, bsz, seq, slopes, qk_gains, lam_params, subln, lam_init):
    cb = qkv.shape[0]
    view = qkv.reshape(cb, bsz, seq, LANES)
    tq, tk = DIFF_TQ, DIFF_TK
    assert tq == tk
    nq = seq // tq
    pairs = [(i, j) for i in range(nq) for j in range(i + 1)]
    qi_tab = jnp.asarray([p[0] for p in pairs], I32)
    ki_tab = jnp.asarray([p[1] for p in pairs], I32)
    g_q = jnp.max(jnp.abs(qk_gains[0].astype(F32)))
    g_k = jnp.max(jnp.abs(qk_gains[1].astype(F32)))
    bound2 = SCORE_BOUND_MARGIN * HEAD_DIM ** 0.5 * g_q * g_k * LOG2E
    bounded = (2.0 * bound2 < MAX_BOUNDED_EXPONENT).astype(I32).reshape(1)
    consts = jnp.concatenate([slopes, bound2.reshape(1)])
    kern = functools.partial(_attn_c_kernel, lam_init=lam_init)

    def im(fn):
        return lambda b, h, s, qt, kt, bd: fn(b, h, s, qt, kt)

    grid_spec = pltpu.PrefetchScalarGridSpec(
        num_scalar_prefetch=3,
        grid=(bsz, C_HEADS, len(pairs)),
        in_specs=[
            pl.BlockSpec(memory_space=pltpu.SMEM),
            pl.BlockSpec((4, HEAD_DIM), im(lambda b, h, s, qt, kt: (0, 0))),
            pl.BlockSpec((1, 2 * HEAD_DIM), im(lambda b, h, s, qt, kt: (0, 0))),
            pl.BlockSpec((None, None, tq, LANES), im(lambda b, h, s, qt, kt: (h, b, qt[s], 0))),
            pl.BlockSpec((None, None, tk, LANES),
                         im(lambda b, h, s, qt, kt: (C_HEADS + h, b, kt[s], 0))),
            pl.BlockSpec((None, None, tk, LANES),
                         im(lambda b, h, s, qt, kt: (2 * C_HEADS + h, b, kt[s], 0))),
        ],
        out_specs=pl.BlockSpec((None, tq, LANES), im(lambda b, h, s, qt, kt: (b, qt[s], h))),
        scratch_shapes=[
            pltpu.VMEM((2, tq, 1), F32),
            pltpu.VMEM((2, tq, 1), F32),
            pltpu.VMEM((2, tq, 2 * LANES), F32),
            pltpu.VMEM((2, tq, tk), F32),
        ],
    )
    return pl.pallas_call(
        kern,
        grid_spec=grid_spec,
        out_shape=jax.ShapeDtypeStruct((bsz, seq, C_HEADS * 2 * HEAD_DIM), BF16),
        compiler_params=_cparams(("arbitrary", "arbitrary", "arbitrary")),
        name="attn_diff",
    )(qi_tab, ki_tab, bounded, consts, lam_params, subln.reshape(1, 2 * HEAD_DIM),
      view, view, view)


def _outproj_kernel(o1_ref, o2_ref, w1_ref, w2_ref, x_ref, g_ref, out_ref):
    acc = (jnp.dot(o1_ref[...], w1_ref[...], preferred_element_type=F32)
           + jnp.dot(o2_ref[...], w2_ref[...], preferred_element_type=F32))
    out_ref[...] = x_ref[...] + g_ref[0] * acc


def _outproj(o1, cb1, o2, cb2, w, x2, gate, seq):
    n, d = x2.shape
    half = w.shape[0] // 2
    tm, tn = 512, 512
    bsz = gate.shape[0]
    return pl.pallas_call(
        _outproj_kernel,
        grid=(n // tm, d // tn),
        in_specs=[
            pl.BlockSpec((tm, half), lambda i, j: (i, cb1)),
            pl.BlockSpec((tm, half), lambda i, j: (i, cb2)),
            pl.BlockSpec((half, tn), lambda i, j: (0, j)),
            pl.BlockSpec((half, tn), lambda i, j: (1, j)),
            pl.BlockSpec((tm, tn), lambda i, j: (i, j)),
            pl.BlockSpec((1, 1, tn), lambda i, j: (i * tm // seq, 0, j)),
        ],
        out_specs=pl.BlockSpec((tm, tn), lambda i, j: (i, j)),
        out_shape=jax.ShapeDtypeStruct((n, d), F32),
        compiler_params=_cparams(("arbitrary", "arbitrary")),
        name="outproj",
    )(o1, o2, w, w, x2, gate.reshape(bsz, 1, d))


PACK_SUB = 8
U32 = jnp.uint32
HIGH_HALF = np.uint32(0xFFFF0000)


def _pack_rows(vals, dst_ref, rows):
    half = vals.shape[1] // 2
    lo = lax.bitcast_convert_type(vals[:, :half].astype(BF16).astype(F32), U32)
    hi = lax.bitcast_convert_type(vals[:, half:].astype(BF16).astype(F32), U32)
    words = (lo >> 16) | (hi & HIGH_HALF)
    for s in range(PACK_SUB):
        dst_ref[pl.ds(s, rows, stride=PACK_SUB), :] = words[:, s * LANES:(s + 1) * LANES]


def _unpack_words(words):
    lo = lax.bitcast_convert_type(words << 16, F32)
    hi = lax.bitcast_convert_type(words & HIGH_HALF, F32)
    return lo, hi


def _unpack_rows(src_ref, rows):
    los, his = [], []
    for s in range(PACK_SUB):
        lo, hi = _unpack_words(src_ref[pl.ds(s, rows, stride=PACK_SUB), :])
        los.append(lo)
        his.append(hi)
    return jnp.concatenate(los + his, axis=1)


def _route_kernel(x_ref, g_ref, sh_ref, sc_ref, wr_ref, rb_ref,
                  h_ref, e8_ref, pos8_ref, gate8_ref, cnt_ref,
                  carry, tri_tok, tri_exp):
    i = pl.program_id(0)
    tm = ROUTE_TM
    per_group = N_EXPERTS // N_GROUPS

    @pl.when(i == 0)
    def _():
        carry[...] = jnp.zeros(carry.shape, F32)
        r = lax.broadcasted_iota(I32, (tm, tm), 0)
        c = lax.broadcasted_iota(I32, (tm, tm), 1)
        tri_tok[...] = jnp.where(r <= c, 1.0, 0.0).astype(BF16)
        re = lax.broadcasted_iota(I32, (N_EXPERTS, N_EXPERTS), 0)
        ce = lax.broadcasted_iota(I32, (N_EXPERTS, N_EXPERTS), 1)
        tri_exp[...] = jnp.where(ce < re, 1.0, 0.0).astype(BF16)

    h = _norm_modulate(x_ref[...], g_ref[...], sh_ref[0], sc_ref[0])
    _pack_rows(h, h_ref, tm)
    logits = lax.dot_general(wr_ref[...], h, (((1,), (1,)), ((), ())), precision=HIGHEST,
                             preferred_element_type=F32)
    scores = jax.nn.sigmoid(logits)
    sel = scores + rb_ref[...]

    sel3 = sel.reshape(N_GROUPS, per_group, tm)
    within = lax.broadcasted_iota(I32, sel3.shape, 1)
    max1 = jnp.max(sel3, axis=1, keepdims=True)
    first = jnp.min(jnp.where(sel3 == max1, within, per_group), axis=1, keepdims=True)
    max2 = jnp.max(jnp.where(within == first, -jnp.inf, sel3), axis=1, keepdims=True)
    gscore = max1 + max2

    gidx = lax.broadcasted_iota(I32, gscore.shape, 0)
    grank = jnp.zeros(gscore.shape, I32)
    for g in range(N_GROUPS):
        other = gscore[g:g + 1]
        beats = (other > gscore) | ((other == gscore) & (gidx > g))
        grank = grank + jnp.where(beats, 1, 0)
    keep = grank < TOPK_GROUPS
    masked = jnp.where(keep, sel3, NEG_INF).reshape(N_EXPERTS, tm)

    eidx = lax.broadcasted_iota(I32, (N_EXPERTS, tm), 0)
    erank = jnp.zeros((N_EXPERTS, tm), I32)
    for e in range(N_EXPERTS):
        other = masked[e:e + 1, :]
        beats = (other > masked) | ((other == masked) & (eidx > e))
        erank = erank + jnp.where(beats, 1, 0)
    chosen = erank < TOP_K

    gates = jnp.where(chosen, scores, 0.0)
    gates = gates / jnp.sum(gates, axis=0, keepdims=True) * ROUTED_SCALE

    chosen_b = jnp.where(chosen, 1.0, 0.0).astype(BF16)
    slot = jnp.dot(tri_exp[...], chosen_b, preferred_element_type=F32)
    csum = jnp.dot(chosen_b, tri_tok[...], preferred_element_type=F32)
    pos = carry[:, 0:1] + csum - 1.0
    eidx_f = eidx.astype(F32)
    e_rows, p_rows, g_rows = [], [], []
    for kk in range(TOP_K):
        pick = chosen & (slot == float(kk))
        e_rows.append(jnp.sum(jnp.where(pick, eidx_f, 0.0), axis=0, keepdims=True))
        p_rows.append(jnp.sum(jnp.where(pick, pos, 0.0), axis=0, keepdims=True))
        g_rows.append(jnp.sum(jnp.where(pick, gates, 0.0), axis=0, keepdims=True))
    e8_ref[...] = jnp.concatenate(e_rows, axis=0).astype(I32)
    pos8_ref[...] = jnp.concatenate(p_rows, axis=0).astype(I32)
    gate8_ref[...] = jnp.concatenate(g_rows, axis=0)
    total = carry[...] + csum[:, tm - 1:tm]
    carry[...] = total
    cnt_ref[...] = total


def _route(x2, seq, gain, shift, scale, w_router, router_bias):
    n, d = x2.shape
    tm = ROUTE_TM
    bsz = shift.shape[0]
    return pl.pallas_call(
        _route_kernel,
        grid=(n // tm,),
        in_specs=[
            pl.BlockSpec((tm, d), lambda i: (i, 0)),
            pl.BlockSpec((1, d), lambda i: (0, 0)),
            pl.BlockSpec((1, 1, d), lambda i: (i * tm // seq, 0, 0)),
            pl.BlockSpec((1, 1, d), lambda i: (i * tm // seq, 0, 0)),
            pl.BlockSpec((N_EXPERTS, d), lambda i: (0, 0)),
            pl.BlockSpec((N_EXPERTS, 1), lambda i: (0, 0)),
        ],
        out_specs=[
            pl.BlockSpec((tm * PACK_SUB, LANES), lambda i: (i, 0)),
            pl.BlockSpec((TOP_K, tm), lambda i: (0, i)),
            pl.BlockSpec((TOP_K, tm), lambda i: (0, i)),
            pl.BlockSpec((TOP_K, tm), lambda i: (0, i)),
            pl.BlockSpec((N_EXPERTS, LANES), lambda i: (0, 0)),
        ],
        out_shape=[
            jax.ShapeDtypeStruct((n * PACK_SUB, LANES), U32),
            jax.ShapeDtypeStruct((TOP_K, n), I32),
            jax.ShapeDtypeStruct((TOP_K, n), I32),
            jax.ShapeDtypeStruct((TOP_K, n), F32),
            jax.ShapeDtypeStruct((N_EXPERTS, LANES), F32),
        ],
        scratch_shapes=[
            pltpu.VMEM((N_EXPERTS, LANES), F32),
            pltpu.VMEM((tm, tm), BF16),
            pltpu.VMEM((N_EXPERTS, N_EXPERTS), BF16),
        ],
        compiler_params=_cparams(("arbitrary",)),
        name="route",
    )(x2, gain.reshape(1, d), shift.reshape(bsz, 1, d), scale.reshape(bsz, 1, d),
      w_router.T, router_bias.reshape(N_EXPERTS, 1))


def _tables_kernel(cnt_ref, e8_ref, pos8_ref, dest_ref, bexp_ref, ends_ref, *, nb_pad):
    counts = cnt_ref[...]
    blocks = jnp.ceil(counts / float(MOE_BM))
    re = lax.broadcasted_iota(I32, (N_EXPERTS, N_EXPERTS), 0)
    ce = lax.broadcasted_iota(I32, (N_EXPERTS, N_EXPERTS), 1)
    tri = jnp.where(ce < re, 1.0, 0.0).astype(BF16)
    start_blk = jnp.dot(tri, blocks.astype(BF16), preferred_element_type=F32)
    end_blk = start_blk + blocks
    e8 = e8_ref[...]
    dest = pos8_ref[...].astype(F32)
    for e in range(N_EXPERTS):
        dest = dest + jnp.where(e8 == e, start_blk[e:e + 1, 0:1] * float(MOE_BM), 0.0)
    dest_ref[...] = dest.astype(I32)
    bidx = lax.broadcasted_iota(I32, (N_EXPERTS, nb_pad), 1).astype(F32)
    owner = jnp.sum(jnp.where(end_blk[:, 0:1] <= bidx, 1.0, 0.0), axis=0, keepdims=True)
    bexp_ref[...] = jnp.minimum(owner, float(N_EXPERTS - 1)).astype(I32)
    ends_ref[...] = end_blk.astype(I32)


def _tables(counts, e8, pos8, nb):
    n = e8.shape[1]
    tm = 2048
    nb_pad = -(-nb // LANES) * LANES
    kern = functools.partial(_tables_kernel, nb_pad=nb_pad)
    return pl.pallas_call(
        kern,
        grid=(n // tm,),
        in_specs=[
            pl.BlockSpec((N_EXPERTS, LANES), lambda i: (0, 0)),
            pl.BlockSpec((TOP_K, tm), lambda i: (0, i)),
            pl.BlockSpec((TOP_K, tm), lambda i: (0, i)),
        ],
        out_specs=[
            pl.BlockSpec((TOP_K, tm), lambda i: (0, i)),
            pl.BlockSpec((1, nb_pad), lambda i: (0, 0)),
            pl.BlockSpec((N_EXPERTS, LANES), lambda i: (0, 0)),
        ],
        out_shape=[
            jax.ShapeDtypeStruct((TOP_K, n), I32),
            jax.ShapeDtypeStruct((1, nb_pad), I32),
            jax.ShapeDtypeStruct((N_EXPERTS, LANES), I32),
        ],
        compiler_params=_cparams(("arbitrary",)),
        name="tables",
    )(counts, e8, pos8)


def _dispatch_kernel(ends_ref, tab_hbm, h_hbm, xs_hbm, tab_smem, zero_buf,
                     sem_tab, sem_rows, sem_zero, *, n_tiles, n_blocks):
    i = pl.program_id(0)
    slot = i % 2
    tm = DISP_TM
    blk_rows = MOE_BM * PACK_SUB

    def tab_copy(t, sl):
        return pltpu.make_async_copy(tab_hbm.at[t], tab_smem.at[sl], sem_tab.at[sl])

    def zero_copy(e):
        end = ends_ref[e]
        start = jnp.where(e > 0, ends_ref[jnp.maximum(e - 1, 0)], 0)
        row0 = pl.multiple_of(jnp.maximum(end - 1, 0) * blk_rows, PACK_SUB)
        return end > start, pltpu.make_async_copy(
            zero_buf, xs_hbm.at[pl.ds(row0, blk_rows), :], sem_zero)

    @pl.when(i == 0)
    def _():
        tab_copy(0, 0).start()
        zero_buf[...] = jnp.zeros(zero_buf.shape, U32)

        def start_body(e, carry):
            has_rows, cp = zero_copy(e)

            @pl.when(has_rows)
            def _():
                cp.start()
            return carry

        def wait_body(e, carry):
            has_rows, cp = zero_copy(e)

            @pl.when(has_rows)
            def _():
                cp.wait()
            return carry

        def tail_copy(blk):
            row0 = pl.multiple_of(blk * blk_rows, PACK_SUB)
            return pltpu.make_async_copy(zero_buf, xs_hbm.at[pl.ds(row0, blk_rows), :],
                                         sem_zero)

        def tail_start(blk, carry):
            tail_copy(blk).start()
            return carry

        def tail_wait(blk, carry):
            tail_copy(blk).wait()
            return carry

        n_used = ends_ref[N_EXPERTS - 1]
        lax.fori_loop(0, N_EXPERTS, start_body, 0)
        lax.fori_loop(n_used, n_blocks, tail_start, 0)
        lax.fori_loop(0, N_EXPERTS, wait_body, 0)
        lax.fori_loop(n_used, n_blocks, tail_wait, 0)

    tab_copy(i, slot).wait()

    @pl.when(i + 1 < n_tiles)
    def _():
        tab_copy(i + 1, 1 - slot).start()

    def body(r, carry):
        src = pl.multiple_of((i * tm + r) * PACK_SUB, PACK_SUB)
        for kk in range(TOP_K):
            dst = pl.multiple_of(tab_smem[slot, kk * tm + r] * PACK_SUB, PACK_SUB)
            pltpu.make_async_copy(h_hbm.at[pl.ds(src, PACK_SUB), :],
                                  xs_hbm.at[pl.ds(dst, PACK_SUB), :], sem_rows).start()
        return carry

    lax.fori_loop(0, tm, body, 0, unroll=2)
    n_rows = tm * TOP_K * PACK_SUB
    pltpu.make_async_copy(h_hbm.at[pl.ds(0, n_rows), :], xs_hbm.at[pl.ds(0, n_rows), :],
                          sem_rows).wait()


def _dispatch(ends, tab, h_packed, nb):
    n_tiles = tab.shape[0]
    kern = functools.partial(_dispatch_kernel, n_tiles=n_tiles, n_blocks=nb)
    grid_spec = pltpu.PrefetchScalarGridSpec(
        num_scalar_prefetch=1,
        grid=(n_tiles,),
        in_specs=[pl.BlockSpec(memory_space=pl.ANY), pl.BlockSpec(memory_space=pl.ANY)],
        out_specs=pl.BlockSpec(memory_space=pl.ANY),
        scratch_shapes=[
            pltpu.SMEM((2, TOP_K * DISP_TM), I32),
            pltpu.VMEM((MOE_BM * PACK_SUB, LANES), U32),
            pltpu.SemaphoreType.DMA((2,)),
            pltpu.SemaphoreType.DMA,
            pltpu.SemaphoreType.DMA,
        ],
    )
    return pl.pallas_call(
        kern,
        grid_spec=grid_spec,
        out_shape=jax.ShapeDtypeStruct((nb * MOE_BM * PACK_SUB, LANES), U32),
        compiler_params=_cparams(("arbitrary",)),
        name="dispatch",
    )(ends, tab, h_packed)


def _experts_kernel(bexp_ref, nused_ref, xs_ref, wu_ref, wd_ref, y_ref, wu_bf, wd_bf):
    b = pl.program_id(0)
    nu = nused_ref[0]
    bm = MOE_BM

    prev_expert = bexp_ref[jnp.maximum(b - 1, 0)]

    @pl.when((b == 0) | (bexp_ref[b] != prev_expert))
    def _():
        wu_bf[...] = wu_ref[...].astype(BF16)
        wd_bf[...] = wd_ref[...].astype(BF16)

    @pl.when(b < nu)
    def _():
        x = _unpack_rows(xs_ref, bm).astype(BF16)
        up = jnp.dot(x, wu_bf[...], preferred_element_type=F32)
        act = _silu(up[:, :EXPERT_DIM]) * up[:, EXPERT_DIM:]
        y = jnp.dot(act.astype(BF16), wd_bf[...], preferred_element_type=F32)
        _pack_rows(y, y_ref, bm)

    @pl.when(b >= nu)
    def _():
        y_ref[...] = jnp.zeros(y_ref.shape, U32)


def _experts(bexp, nused, xs, w_up_all, w_down_all, layer):
    bm = MOE_BM
    nb = xs.shape[0] // (bm * PACK_SUB)
    d = w_up_all.shape[2]
    grid_spec = pltpu.PrefetchScalarGridSpec(
        num_scalar_prefetch=2,
        grid=(nb,),
        in_specs=[
            pl.BlockSpec((bm * PACK_SUB, LANES),
                         lambda b, be, nu: (jnp.minimum(b, nu[0] - 1), 0)),
            pl.BlockSpec((None, None, d, 2 * EXPERT_DIM),
                         lambda b, be, nu: (layer, be[b], 0, 0)),
            pl.BlockSpec((None, None, EXPERT_DIM, d),
                         lambda b, be, nu: (layer, be[b], 0, 0)),
        ],
        out_specs=pl.BlockSpec((bm * PACK_SUB, LANES), lambda b, be, nu: (b, 0)),
        scratch_shapes=[
            pltpu.VMEM((d, 2 * EXPERT_DIM), BF16),
            pltpu.VMEM((EXPERT_DIM, d), BF16),
        ],
    )
    return pl.pallas_call(
        _experts_kernel,
        grid_spec=grid_spec,
        out_shape=jax.ShapeDtypeStruct((nb * bm * PACK_SUB, LANES), U32),
        compiler_params=_cparams(("arbitrary",)),
        name="experts",
    )(bexp, nused, xs, w_up_all, w_down_all)


def _combine_kernel(tab_hbm, y_hbm, gate_ref, h_ref, x_ref, gf_ref, wsu_ref, wsd_ref, out_ref,
                    ybuf, tab_smem, shared_scr, sem_tab, sem_rows, *, n_tiles):
    i = pl.program_id(0)
    slot = i % 2
    tm = COMB_TM

    def tab_copy(t, sl):
        return pltpu.make_async_copy(tab_hbm.at[t], tab_smem.at[sl], sem_tab.at[sl])

    def issue_rows(sl):
        for kk in range(TOP_K):
            def body(r, carry, kk=kk):
                src = pl.multiple_of(tab_smem[sl, kk * tm + r] * PACK_SUB, PACK_SUB)
                dst = pl.multiple_of(r * PACK_SUB, PACK_SUB)
                pltpu.make_async_copy(y_hbm.at[pl.ds(src, PACK_SUB), :],
                                      ybuf.at[sl, kk, pl.ds(dst, PACK_SUB), :],
                                      sem_rows.at[sl]).start()
                return carry
            lax.fori_loop(0, tm, body, 0, unroll=8)

    @pl.when(i == 0)
    def _():
        tab_copy(0, 0).start()
        tab_copy(0, 0).wait()
        issue_rows(0)
        if n_tiles > 1:
            tab_copy(1, 1).start()

    @pl.when(i + 1 < n_tiles)
    def _():
        tab_copy(i + 1, 1 - slot).wait()
        issue_rows(1 - slot)

    @pl.when(i + 2 < n_tiles)
    def _():
        tab_copy(i + 2, slot).start()

    hb = _unpack_rows(h_ref, tm).astype(BF16)
    up = jnp.dot(hb, wsu_ref[...], preferred_element_type=F32)
    half = wsd_ref.shape[0]
    act = _silu(up[:, :half]) * up[:, half:]
    shared_scr[...] = jnp.dot(act.astype(BF16), wsd_ref[...], preferred_element_type=F32)

    for kk in range(TOP_K):
        pltpu.make_async_copy(y_hbm.at[pl.ds(0, tm * PACK_SUB), :], ybuf.at[slot, kk],
                              sem_rows.at[slot]).wait()
    gate = gate_ref[...]
    gates = [jnp.broadcast_to(gate[:, kk:kk + 1], (tm, LANES)) for kk in range(TOP_K)]
    d_half = PACK_SUB * LANES
    for s in range(PACK_SUB):
        acc_lo = jnp.zeros((tm, LANES), F32)
        acc_hi = jnp.zeros((tm, LANES), F32)
        for kk in range(TOP_K):
            lo, hi = _unpack_words(ybuf[slot, kk, pl.ds(s, tm, stride=PACK_SUB), :])
            acc_lo = acc_lo + gates[kk] * lo
            acc_hi = acc_hi + gates[kk] * hi
        for acc, c0 in ((acc_lo, s * LANES), (acc_hi, d_half + s * LANES)):
            cs = slice(c0, c0 + LANES)
            out_ref[:, cs] = x_ref[:, cs] + gf_ref[0, :, cs] * (acc + shared_scr[:, cs])


def _combine(tab, y, gate_t, h, x2, g_f, w_su, w_sd, seq):
    n, d = x2.shape
    tm = COMB_TM
    n_tiles = n // tm
    bsz = g_f.shape[0]
    kern = functools.partial(_combine_kernel, n_tiles=n_tiles)
    return pl.pallas_call(
        kern,
        grid=(n_tiles,),
        in_specs=[
            pl.BlockSpec(memory_space=pl.ANY),
            pl.BlockSpec(memory_space=pl.ANY),
            pl.BlockSpec((tm, TOP_K), lambda i: (i, 0)),
            pl.BlockSpec((tm * PACK_SUB, LANES), lambda i: (i, 0)),
            pl.BlockSpec((tm, d), lambda i: (i, 0)),
            pl.BlockSpec((1, 1, d), lambda i: (i * tm // seq, 0, 0)),
            pl.BlockSpec(w_su.shape, lambda i: (0, 0)),
            pl.BlockSpec(w_sd.shape, lambda i: (0, 0)),
        ],
        out_specs=pl.BlockSpec((tm, d), lambda i: (i, 0)),
        out_shape=jax.ShapeDtypeStruct((n, d), F32),
        scratch_shapes=[
            pltpu.VMEM((2, TOP_K, tm * PACK_SUB, LANES), U32),
            pltpu.SMEM((2, TOP_K * tm), I32),
            pltpu.VMEM((tm, d), F32),
            pltpu.SemaphoreType.DMA((2,)),
            pltpu.SemaphoreType.DMA((2,)),
        ],
        compiler_params=_cparams(("arbitrary",)),
        name="combine",
    )(tab, y, gate_t, h, x2, g_f.reshape(bsz, 1, d), w_su, w_sd)


def _tile_table(dest8, tm):
    n = dest8.shape[1]
    return dest8.reshape(TOP_K, n // tm, tm).transpose(1, 0, 2).reshape(n // tm, TOP_K * tm)


def _moe(x2, seq, gain, shift, scale, g_f, w_router, router_bias, w_up_all, w_down_all, layer,
         w_su, w_sd):
    n, d = x2.shape
    assert d == 2 * PACK_SUB * LANES
    h, e8, pos8, gate8, counts = _route(x2, seq, gain, shift, scale, w_router, router_bias)
    nb = -(-(n * TOP_K + N_EXPERTS * (MOE_BM - 1)) // MOE_BM)
    dest8, bexp, ends = _tables(counts, e8, pos8, nb)
    ends = ends[:, 0]
    xs = _dispatch(ends, _tile_table(dest8, DISP_TM), h, nb)
    y = _experts(bexp[0, :nb], ends[N_EXPERTS - 1:], xs, w_up_all, w_down_all, layer)
    return _combine(_tile_table(dest8, COMB_TM), y, gate8.T, h, x2, g_f, w_su.astype(BF16),
                    w_sd.astype(BF16), seq)


def _alibi_slopes(nh):
    return jnp.asarray(2.0 ** (-8.0 * np.arange(1, nh + 1) / nh), dtype=F32)


def _even_weights(w_in, qk_norm):
    a_w = A_HEADS * HEAD_DIM
    b_w = B_HEADS * HEAD_DIM
    kv_w = B_KV_HEADS * HEAD_DIM
    cuts = np.cumsum([a_w, a_w, a_w, b_w, kv_w])
    qa, ka, va, qb, kb, vb = jnp.split(w_in, cuts, axis=1)

    def dup(a):
        return jnp.concatenate([a[:, hh * HEAD_DIM:(hh + 1) * HEAD_DIM]
                                for hh in range(B_KV_HEADS) for _ in range(2)], axis=1)

    w = jnp.concatenate([qa, ka, qb, dup(kb), va, dup(vb)], axis=1).astype(BF16)
    q_scale = HEAD_DIM ** -0.5
    gains = jnp.concatenate([
        jnp.tile(qk_norm[0] * q_scale, A_HEADS), jnp.tile(qk_norm[1], A_HEADS),
        jnp.tile(qk_norm[2] * q_scale, B_HEADS), jnp.tile(qk_norm[3], 2 * B_KV_HEADS),
        jnp.ones((a_w + 2 * kv_w,), F32)])
    n_norm = 2 * a_w + b_w + 2 * kv_w
    return w, gains, n_norm


def kernel(x, c, norm_mix, norm_ffn, w_ada, b_ada, ab_w_in, ab_qk_norm, ab_sinks, ab_w_out,
           c_w_in, c_qk_norm, c_lambda, c_subln, c_w_out, w_router, router_bias,
           experts_up, experts_down, shared_up, shared_down):
    bsz, seq, d = x.shape
    depth = w_ada.shape[0]
    n = bsz * seq
    ada = _ada(c, w_ada, b_ada)
    x2 = x.reshape(n, d)
    for i in range(depth):
        sh_m, sc_m, g_m, sh_f, sc_f, g_f = jnp.split(ada[i], 6, axis=-1)
        j = i // 2
        if i % 2 == 0:
            w, gains, n_norm = _even_weights(ab_w_in[j], ab_qk_norm[j])
            qkv = _proj(x2, seq, norm_mix[i], sh_m, sc_m, w, gains, n_norm)
            pa = A_HEADS // 2
            pb = B_HEADS // 2
            q_a, k_a, q_b, k_b = 0, pa, 2 * pa, 2 * pa + pb
            v_a = k_b + B_KV_HEADS
            v_b = v_a + pa
            o_a = _attn_a(qkv, bsz, seq, q_a, k_a, v_a, _alibi_slopes(A_HEADS))
            o_b = _attn_b(qkv, bsz, seq, q_b, k_b, v_b, _alibi_slopes(B_HEADS),
                          ab_sinks[j].astype(F32))
            x2 = _outproj(o_a.reshape(n, -1), 0, o_b.reshape(n, -1), 0,
                          ab_w_out[j].astype(BF16), x2, g_m, seq)
        else:
            q_scale = HEAD_DIM ** -0.5 * LOG2E
            gains = jnp.concatenate([
                jnp.tile(c_qk_norm[j, 0] * q_scale, 2 * C_HEADS),
                jnp.tile(c_qk_norm[j, 1], 2 * C_HEADS),
                jnp.ones((2 * C_HEADS * HEAD_DIM,), F32)])
            qkv = _proj(x2, seq, norm_mix[i], sh_m, sc_m, c_w_in[j].astype(BF16), gains,
                        4 * C_HEADS * HEAD_DIM)
            lam_init = 0.8 - 0.6 * math.exp(-0.3 * i)
            o_c = _attn_c(qkv, bsz, seq, _alibi_slopes(C_HEADS), c_qk_norm[j],
                          c_lambda[j].astype(F32),
                          c_subln[j].astype(F32), lam_init).reshape(n, -1)
            x2 = _outproj(o_c, 0, o_c, 1, c_w_out[j].astype(BF16), x2, g_m, seq)
        x2 = _moe(x2, seq, norm_ffn[i], sh_f, sc_f, g_f, w_router[i], router_bias[i],
                  experts_up, experts_down, i, shared_up[i], shared_down[i])
    return x2.reshape(bsz, seq, d)
```

```python
import functools
import math

import jax
import jax.numpy as jnp
import numpy as np
from jax import lax
from jax.experimental import pallas as pl
from jax.experimental.pallas import tpu as pltpu

F32 = jnp.float32
BF16 = jnp.bfloat16
I32 = jnp.int32
HIGHEST = lax.Precision.HIGHEST

HEAD_DIM = 64
A_HEADS = 16
DILATIONS = (1, 4, 16)
A_MAX_DIST = 128
B_HEADS = 16
B_KV_HEADS = 2
B_MAX_DIST = 127
C_HEADS = 16
N_EXPERTS = 64
N_GROUPS = 8
TOPK_GROUPS = 4
TOP_K = 8
EXPERT_DIM = 512
ROUTED_SCALE = 2.5
RMS_EPS = 1e-6
NEG_INF = -1e30
LOG2E = math.log2(math.e)
SCORE_BOUND_MARGIN = 1.02
MAX_BOUNDED_EXPONENT = 100.0

LANES = 128
VMEM_LIMIT = 56 * 1024 * 1024

PROJ_TM = 512
PROJ_TN = 512
ATT_TILE = 2048
BAND = 128
DIFF_TQ = 512
DIFF_TK = 512
ROUTE_TM = 512
MOE_BM = 512
COMB_TM = 256
DISP_TM = 512


def _cparams(sem):
    return pltpu.CompilerParams(dimension_semantics=sem, vmem_limit_bytes=VMEM_LIMIT)


def _silu(x):
    return x * jax.nn.sigmoid(x)


def _ada_kernel(c_ref, w_ref, b_ref, o_ref):
    cond = _silu(c_ref[...])
    o_ref[...] = jnp.dot(cond, w_ref[...], precision=HIGHEST,
                         preferred_element_type=F32) + b_ref[...]


def _ada(c, w_ada, b_ada):
    depth, d, d6 = w_ada.shape
    b = c.shape[0]
    bp = 8
    cp = jnp.zeros((bp, d), F32).at[:b].set(c)
    tn = 1024
    out = pl.pallas_call(
        _ada_kernel,
        grid=(depth, d6 // tn),
        in_specs=[
            pl.BlockSpec((bp, d), lambda i, j: (0, 0)),
            pl.BlockSpec((None, d, tn), lambda i, j: (i, 0, j)),
            pl.BlockSpec((None, 1, tn), lambda i, j: (i, 0, j)),
        ],
        out_specs=pl.BlockSpec((None, bp, tn), lambda i, j: (i, 0, j)),
        out_shape=jax.ShapeDtypeStruct((depth, bp, d6), F32),
        compiler_params=_cparams(("arbitrary", "arbitrary")),
        name="ada",
    )(cp, w_ada, b_ada.reshape(depth, 1, d6))
    return out[:, :b]


def _norm_modulate(x, g, sh, sc):
    ms = jnp.mean(x * x, axis=-1, keepdims=True)
    y = x * lax.rsqrt(ms + RMS_EPS) * g
    return y * (1.0 + sc) + sh


def _proj_kernel(x_ref, g_ref, sh_ref, sc_ref, w_ref, hg_ref, pm_ref, o_ref, h_scr, *,
                 n_norm_cols, tn):
    j = pl.program_id(1)

    @pl.when(j == 0)
    def _():
        h = _norm_modulate(x_ref[...], g_ref[...], sh_ref[0], sc_ref[0])
        h_scr[...] = h.astype(BF16)

    acc = jnp.dot(h_scr[...], w_ref[...], preferred_element_type=F32)
    for cb in range(tn // LANES):
        a = acc[:, cb * LANES:(cb + 1) * LANES]
        is_norm = j * tn + cb * LANES < n_norm_cols

        @pl.when(is_norm)
        def _():
            sq = a * a
            hi = sq.astype(BF16)
            lo = (sq - hi.astype(F32)).astype(BF16)
            ms = (jnp.dot(hi, pm_ref[...], preferred_element_type=F32)
                  + jnp.dot(lo, pm_ref[...], preferred_element_type=F32))
            y = a * lax.rsqrt(ms + RMS_EPS) * hg_ref[:, cb * LANES:(cb + 1) * LANES]
            o_ref[cb] = y.astype(BF16)

        @pl.when(jnp.logical_not(is_norm))
        def _():
            o_ref[cb] = a.astype(BF16)


def _head_mean_matrix():
    r = np.arange(LANES)
    pm = (r[:, None] // HEAD_DIM == r[None, :] // HEAD_DIM).astype(np.float32) / HEAD_DIM
    return jnp.asarray(pm, BF16)


def _proj(x2, seq, gain, shift, scale, w, head_gain, n_norm_cols):
    n, d = x2.shape
    cols = w.shape[1]
    tm, tn = PROJ_TM, PROJ_TN
    bsz = shift.shape[0]
    kern = functools.partial(_proj_kernel, n_norm_cols=n_norm_cols, tn=tn)
    return pl.pallas_call(
        kern,
        grid=(n // tm, cols // tn),
        in_specs=[
            pl.BlockSpec((tm, d), lambda i, j: (i, 0)),
            pl.BlockSpec((1, d), lambda i, j: (0, 0)),
            pl.BlockSpec((1, 1, d), lambda i, j: (i * tm // seq, 0, 0)),
            pl.BlockSpec((1, 1, d), lambda i, j: (i * tm // seq, 0, 0)),
            pl.BlockSpec((d, tn), lambda i, j: (0, j)),
            pl.BlockSpec((1, tn), lambda i, j: (0, j)),
            pl.BlockSpec((LANES, LANES), lambda i, j: (0, 0)),
        ],
        out_specs=pl.BlockSpec((tn // LANES, tm, LANES), lambda i, j: (j, i, 0)),
        out_shape=jax.ShapeDtypeStruct((cols // LANES, n, LANES), BF16),
        scratch_shapes=[pltpu.VMEM((tm, d), BF16)],
        compiler_params=_cparams(("arbitrary", "arbitrary")),
        name="proj",
    )(x2, gain.reshape(1, d), shift.reshape(bsz, 1, d), scale.reshape(bsz, 1, d), w,
      head_gain.reshape(1, cols), _head_mean_matrix())


def _band_block(qz, kk, v_aug, bias, sink=None):
    s = lax.dot_general(qz, kk, (((1,), (1,)), ((), ())), preferred_element_type=F32) + bias
    m = jnp.max(s, axis=-1, keepdims=True)
    if sink is not None:
        m = jnp.maximum(m, sink)
    p = jnp.exp(s - m)
    res = jnp.dot(p.astype(BF16), v_aug, preferred_element_type=F32)
    return res, m


def _band_masks(max_dist):
    qi = lax.broadcasted_iota(I32, (BAND, 2 * BAND), 0)
    kj = lax.broadcasted_iota(I32, (BAND, 2 * BAND), 1)
    dist = BAND + qi - kj
    valid = (dist >= 0) & (dist <= max_dist)
    valid_first = valid & (kj >= BAND)
    return dist.astype(F32), valid, valid_first


def _attn_a_kernel(slopes_ref,
                   q1, q4, q16,
                   k1c, k1p, k4c, k4p, k16c, k16p,
                   v1c, v1p, v4c, v4p, v16c, v16p,
                   o_ref, kcat, vcat, st):
    p = pl.program_id(1)
    n = pl.program_id(2)
    lane = lax.broadcasted_iota(I32, (BAND, LANES), 1)
    distf, valid, valid_first = _band_masks(A_MAX_DIST)
    seq_start = n == 0
    branches = ((1, q1, k1c, k1p, v1c, v1p), (4, q4, k4c, k4p, v4c, v4p),
                (16, q16, k16c, k16p, v16c, v16p))
    for bi, (d, qv, kc, kp, vc, vp) in enumerate(branches):
        n_l = ATT_TILE // BAND // d
        rows = n_l * BAND
        biases = []
        for h in range(2):
            slope = slopes_ref[2 * p + h] * float(d)
            b_full = jnp.where(valid, -slope * distf, NEG_INF)
            b_first = jnp.where(valid_first, -slope * distf, NEG_INF)
            biases.append((b_full, jnp.where(seq_start, b_first, b_full)))
        for r in range(d):
            cs = slice(r * LANES, (r + 1) * LANES)
            kcat[0:BAND, :] = kp[:, cs]
            kcat[BAND:BAND + rows, :] = kc[:, cs]
            vprev = vp[:, cs]
            vcur = vc[:, cs]
            for h in range(2):
                vcat[h, 0:BAND, :] = jnp.where(lane // HEAD_DIM == h, vprev, 1.0).astype(BF16)
                lane_k = lax.broadcasted_iota(I32, (rows, LANES), 1)
                vcat[h, BAND:BAND + rows, :] = jnp.where(
                    lane_k // HEAD_DIM == h, vcur, 1.0).astype(BF16)
            for h in range(2):
                m_lane = LANES - 1 if h == 0 else 0
                for jb in range(n_l):
                    qb = qv[jb * BAND:(jb + 1) * BAND, cs]
                    qz = jnp.where(lane // HEAD_DIM == h, qb, 0.0).astype(BF16)
                    kk = kcat[jb * BAND:jb * BAND + 2 * BAND, :]
                    vv = vcat[h, jb * BAND:jb * BAND + 2 * BAND, :]
                    bias = biases[h][1] if jb == 0 else biases[h][0]
                    res, m = _band_block(qz, kk, vv, bias)
                    res = jnp.where(lane == m_lane, m, res)
                    if d == 1:
                        st[bi, h, jb * BAND:(jb + 1) * BAND, :] = res
                    else:
                        st[bi, h, pl.ds(jb * BAND * d + r, BAND, stride=d), :] = res
    for ib in range(ATT_TILE // BAND):
        rs = slice(ib * BAND, (ib + 1) * BAND)
        for h in range(2):
            m_lane = LANES - 1 if h == 0 else 0
            l_lane = HEAD_DIM if h == 0 else 1
            hs = slice(h * HEAD_DIM, (h + 1) * HEAD_DIM)
            parts = [st[bi, h, rs, :] for bi in range(3)]
            ms = [x[:, m_lane:m_lane + 1] for x in parts]
            m_all = jnp.maximum(jnp.maximum(ms[0], ms[1]), ms[2])
            num = sum(jnp.exp(mm - m_all) * x for mm, x in zip(ms, parts))
            out = num / num[:, l_lane:l_lane + 1]
            o_ref[rs, hs] = out[:, hs].astype(BF16)


def _attn_a(qkv, bsz, seq, q_cb, k_cb, v_cb, slopes):
    cb = qkv.shape[0]
    views = {d: qkv.reshape(cb, bsz, seq // d, d * LANES) for d in DILATIONS}
    n_pairs = A_HEADS // 2
    tile = ATT_TILE
    in_specs = [pl.BlockSpec(memory_space=pltpu.SMEM)]
    args = [slopes]
    for d in DILATIONS:
        in_specs.append(pl.BlockSpec((None, None, tile // d, d * LANES),
                                     lambda b, p, n: (q_cb + p, b, n, 0)))
        args.append(views[d])
    for base in (k_cb, v_cb):
        for d in DILATIONS:
            per = tile // d // BAND
            in_specs.append(pl.BlockSpec((None, None, tile // d, d * LANES),
                                         lambda b, p, n, base=base: (base + p, b, n, 0)))
            in_specs.append(pl.BlockSpec(
                (None, None, BAND, d * LANES),
                lambda b, p, n, base=base, per=per: (base + p, b, jnp.maximum(n * per - 1, 0), 0)))
            args += [views[d], views[d]]
    return pl.pallas_call(
        _attn_a_kernel,
        grid=(bsz, n_pairs, seq // tile),
        in_specs=in_specs,
        out_specs=pl.BlockSpec((None, tile, LANES), lambda b, p, n: (b, n, p)),
        out_shape=jax.ShapeDtypeStruct((bsz, seq, A_HEADS * HEAD_DIM), BF16),
        scratch_shapes=[
            pltpu.VMEM((BAND + tile, LANES), BF16),
            pltpu.VMEM((2, BAND + tile, LANES), BF16),
            pltpu.VMEM((3, 2, tile, LANES), F32),
        ],
        compiler_params=_cparams(("arbitrary", "arbitrary", "arbitrary")),
        name="attn_dilated",
    )(*args)


def _attn_b_kernel(slopes_ref, sinks_ref, q_ref, kc, kp, vc, vp, o_ref, kcat, vcat):
    p = pl.program_id(1)
    n = pl.program_id(2)
    lane = lax.broadcasted_iota(I32, (BAND, LANES), 1)
    lane_k = lax.broadcasted_iota(I32, (ATT_TILE, LANES), 1)
    distf, valid, valid_first = _band_masks(B_MAX_DIST)
    seq_start = n == 0
    n_l = ATT_TILE // BAND
    kcat[0:BAND, :] = kp[...]
    kcat[BAND:, :] = kc[...]
    vprev = vp[...]
    vcur = vc[...]
    for h in range(2):
        vcat[h, 0:BAND, :] = jnp.where(lane // HEAD_DIM == h, vprev, 1.0).astype(BF16)
        vcat[h, BAND:, :] = jnp.where(lane_k // HEAD_DIM == h, vcur, 1.0).astype(BF16)
    for h in range(2):
        slope = slopes_ref[2 * p + h]
        sink = sinks_ref[2 * p + h]
        b_full = jnp.where(valid, -slope * distf, NEG_INF)
        b_first = jnp.where(seq_start, jnp.where(valid_first, -slope * distf, NEG_INF), b_full)
        l_lane = HEAD_DIM if h == 0 else 0
        for jb in range(n_l):
            qb = q_ref[jb * BAND:(jb + 1) * BAND, :]
            qz = jnp.where(lane // HEAD_DIM == h, qb, 0.0).astype(BF16)
            kk = kcat[jb * BAND:jb * BAND + 2 * BAND, :]
            vv = vcat[h, jb * BAND:jb * BAND + 2 * BAND, :]
            res, m = _band_block(qz, kk, vv, b_first if jb == 0 else b_full, sink=sink)
            denom = res[:, l_lane:l_lane + 1] + jnp.exp(sink - m)
            hs = slice(h * HEAD_DIM, (h + 1) * HEAD_DIM)
            o_ref[jb * BAND:(jb + 1) * BAND, hs] = (res / denom)[:, hs].astype(BF16)


def _attn_b(qkv, bsz, seq, q_cb, k_cb, v_cb, slopes, sinks):
    cb = qkv.shape[0]
    view = qkv.reshape(cb, bsz, seq, LANES)
    n_pairs = B_HEADS // 2
    pairs_per_kv = n_pairs // B_KV_HEADS
    tile = ATT_TILE
    per = tile // BAND

    def cur(base):
        return pl.BlockSpec((None, None, tile, LANES),
                            lambda b, p, n: (base + p // pairs_per_kv, b, n, 0))

    def prev(base):
        return pl.BlockSpec(
            (None, None, BAND, LANES),
            lambda b, p, n: (base + p // pairs_per_kv, b, jnp.maximum(n * per - 1, 0), 0))

    return pl.pallas_call(
        _attn_b_kernel,
        grid=(bsz, n_pairs, seq // tile),
        in_specs=[
            pl.BlockSpec(memory_space=pltpu.SMEM),
            pl.BlockSpec(memory_space=pltpu.SMEM),
            pl.BlockSpec((None, None, tile, LANES), lambda b, p, n: (q_cb + p, b, n, 0)),
            cur(k_cb), prev(k_cb), cur(v_cb), prev(v_cb),
        ],
        out_specs=pl.BlockSpec((None, tile, LANES), lambda b, p, n: (b, n, p)),
        out_shape=jax.ShapeDtypeStruct((bsz, seq, B_HEADS * HEAD_DIM), BF16),
        scratch_shapes=[
            pltpu.VMEM((BAND + tile, LANES), BF16),
            pltpu.VMEM((2, BAND + tile, LANES), BF16),
        ],
        compiler_params=_cparams(("arbitrary", "arbitrary", "arbitrary")),
        name="attn_swa",
    )(slopes, sinks, view, view, view, view, view)


def _attn_c_kernel(qi_tab, ki_tab, bounded_ref, consts_ref, lam_ref, subln_ref,
                   q_ref, k_ref, v_ref, o_ref, m_scr, l_scr, acc_scr, bias_scr, *, lam_init):
    h = pl.program_id(1)
    step = pl.program_id(2)
    qi = qi_tab[step]
    ki = ki_tab[step]
    tq, tk = DIFF_TQ, DIFF_TK
    slope2 = consts_ref[h] * LOG2E
    bound2 = consts_ref[C_HEADS]
    bounded = bounded_ref[0] == 1
    tile_off = slope2 * ((ki - qi) * tk).astype(F32)

    @pl.when(step == 0)
    def _():
        rel = (lax.broadcasted_iota(I32, (tq, tk), 1)
               - lax.broadcasted_iota(I32, (tq, tk), 0)).astype(F32)
        bias = slope2 * rel
        bias_scr[0] = bias
        bias_scr[1] = jnp.where(rel <= 0.0, bias, NEG_INF)

    @pl.when(ki == 0)
    def _():
        m_scr[...] = jnp.full(m_scr.shape, NEG_INF, F32)
        l_scr[...] = jnp.zeros(l_scr.shape, F32)
        acc_scr[...] = jnp.zeros(acc_scr.shape, F32)

    def scores(a, q, k, lane):
        qz = jnp.where(lane // HEAD_DIM == a, q, 0.0).astype(BF16)
        return lax.dot_general(qz, k, (((1,), (1,)), ((), ())), preferred_element_type=F32)

    def update_online(bias_idx):
        q = q_ref[...]
        k = k_ref[...]
        v = v_ref[...]
        lane = lax.broadcasted_iota(I32, (tq, LANES), 1)
        for a in range(2):
            s = scores(a, q, k, lane) + bias_scr[bias_idx]
            m_prev = m_scr[a]
            m_new = jnp.maximum(m_prev, jnp.max(s, axis=-1, keepdims=True) + tile_off)
            alpha = jnp.exp2(m_prev - m_new)
            pr = jnp.exp2(s - (m_new - tile_off))
            l_scr[a] = alpha * l_scr[a] + jnp.sum(pr, axis=-1, keepdims=True)
            acc_scr[a, :, 0:LANES] = alpha * acc_scr[a, :, 0:LANES] + jnp.dot(
                pr.astype(BF16), v, preferred_element_type=F32)
            m_scr[a] = m_new

    def update_bounded(bias_idx):
        q = q_ref[...]
        k = k_ref[...]
        v = v_ref[...]
        lane = lax.broadcasted_iota(I32, (tq, LANES), 1)
        bias = bias_scr[bias_idx] + (tile_off - bound2)
        v_aug = jnp.concatenate([v, jnp.ones_like(v)], axis=1)
        for a in range(2):
            pr = jnp.exp2(scores(a, q, k, lane) + bias).astype(BF16)
            acc_scr[a] = acc_scr[a] + jnp.dot(pr, v_aug, preferred_element_type=F32)

    def finish(o1, o2):
        lp = lam_ref[...]
        lam = (jnp.exp(jnp.sum(lp[0:1] * lp[1:2], axis=-1, keepdims=True))
               - jnp.exp(jnp.sum(lp[2:3] * lp[3:4], axis=-1, keepdims=True)) + lam_init)
        o = o1 - lam * o2
        ms = jnp.mean(o * o, axis=-1, keepdims=True)
        o = o * lax.rsqrt(ms + RMS_EPS) * subln_ref[...] * (1.0 - lam_init)
        o_ref[...] = o.astype(BF16)

    @pl.when(bounded & (ki < qi))
    def _():
        update_bounded(0)

    @pl.when(bounded & (ki == qi))
    def _():
        update_bounded(1)
        finish(acc_scr[0, :, 0:LANES] / acc_scr[0, :, LANES:],
               acc_scr[1, :, 0:LANES] / acc_scr[1, :, LANES:])

    @pl.when(jnp.logical_not(bounded) & (ki < qi))
    def _():
        update_online(0)

    @pl.when(jnp.logical_not(bounded) & (ki == qi))
    def _():
        update_online(1)
        finish(acc_scr[0, :, 0:LANES] / l_scr[0], acc_scr[1, :, 0:LANES] / l_scr[1])


def _attn_c(qkv, bsz, seq, slopes, qk_gains, lam_params, subln, lam_init):
    cb = qkv.shape[0]
    view = qkv.reshape(cb, bsz, seq, LANES)
    tq, tk = DIFF_TQ, DIFF_TK
    assert tq == tk
    nq = seq // tq
    pairs = [(i, j) for i in range(nq) for j in range(i + 1)]
    qi_tab = jnp.asarray([p[0] for p in pairs], I32)
    ki_tab = jnp.asarray([p[1] for p in pairs], I32)
    g_q = jnp.max(jnp.abs(qk_gains[0].astype(F32)))
    g_k = jnp.max(jnp.abs(qk_gains[1].astype(F32)))
    bound2 = SCORE_BOUND_MARGIN * HEAD_DIM ** 0.5 * g_q * g_k * LOG2E
    bounded = (2.0 * bound2 < MAX_BOUNDED_EXPONENT).astype(I32).reshape(1)
    consts = jnp.concatenate([slopes, bound2.reshape(1)])
    kern = functools.partial(_attn_c_kernel, lam_init=lam_init)

    def im(fn):
        return lambda b, h, s, qt, kt, bd: fn(b, h, s, qt, kt)

    grid_spec = pltpu.PrefetchScalarGridSpec(
        num_scalar_prefetch=3,
        grid=(bsz, C_HEADS, len(pairs)),
        in_specs=[
            pl.BlockSpec(memory_space=pltpu.SMEM),
            pl.BlockSpec((4, HEAD_DIM), im(lambda b, h, s, qt, kt: (0, 0))),
            pl.BlockSpec((1, 2 * HEAD_DIM), im(lambda b, h, s, qt, kt: (0, 0))),
            pl.BlockSpec((None, None, tq, LANES), im(lambda b, h, s, qt, kt: (h, b, qt[s], 0))),
            pl.BlockSpec((None, None, tk, LANES),
                         im(lambda b, h, s, qt, kt: (C_HEADS + h, b, kt[s], 0))),
            pl.BlockSpec((None, None, tk, LANES),
                         im(lambda b, h, s, qt, kt: (2 * C_HEADS + h, b, kt[s], 0))),
        ],
        out_specs=pl.BlockSpec((None, tq, LANES), im(lambda b, h, s, qt, kt: (b, qt[s], h))),
        scratch_shapes=[
            pltpu.VMEM((2, tq, 1), F32),
            pltpu.VMEM((2, tq, 1), F32),
            pltpu.VMEM((2, tq, 2 * LANES), F32),
            pltpu.VMEM((2, tq, tk), F32),
        ],
    )
    return pl.pallas_call(
        kern,
        grid_spec=grid_spec,
        out_shape=jax.ShapeDtypeStruct((bsz, seq, C_HEADS * 2 * HEAD_DIM), BF16),
        compiler_params=_cparams(("arbitrary", "arbitrary", "arbitrary")),
        name="attn_diff",
    )(qi_tab, ki_tab, bounded, consts, lam_params, subln.reshape(1, 2 * HEAD_DIM),
      view, view, view)


def _outproj_kernel(o1_ref, o2_ref, w1_ref, w2_ref, x_ref, g_ref, out_ref):
    acc = (jnp.dot(o1_ref[...], w1_ref[...], preferred_element_type=F32)
           + jnp.dot(o2_ref[...], w2_ref[...], preferred_element_type=F32))
    out_ref[...] = x_ref[...] + g_ref[0] * acc


def _outproj(o1, cb1, o2, cb2, w, x2, gate, seq):
    n, d = x2.shape
    half = w.shape[0] // 2
    tm, tn = 512, 512
    bsz = gate.shape[0]
    return pl.pallas_call(
        _outproj_kernel,
        grid=(n // tm, d // tn),
        in_specs=[
            pl.BlockSpec((tm, half), lambda i, j: (i, cb1)),
            pl.BlockSpec((tm, half), lambda i, j: (i, cb2)),
            pl.BlockSpec((half, tn), lambda i, j: (0, j)),
            pl.BlockSpec((half, tn), lambda i, j: (1, j)),
            pl.BlockSpec((tm, tn), lambda i, j: (i, j)),
            pl.BlockSpec((1, 1, tn), lambda i, j: (i * tm // seq, 0, j)),
        ],
        out_specs=pl.BlockSpec((tm, tn), lambda i, j: (i, j)),
        out_shape=jax.ShapeDtypeStruct((n, d), F32),
        compiler_params=_cparams(("arbitrary", "arbitrary")),
        name="outproj",
    )(o1, o2, w, w, x2, gate.reshape(bsz, 1, d))


PACK_SUB = 8
U32 = jnp.uint32
HIGH_HALF = np.uint32(0xFFFF0000)


def _pack_rows(vals, dst_ref, rows):
    half = vals.shape[1] // 2
    lo = lax.bitcast_convert_type(vals[:, :half].astype(BF16).astype(F32), U32)
    hi = lax.bitcast_convert_type(vals[:, half:].astype(BF16).astype(F32), U32)
    words = (lo >> 16) | (hi & HIGH_HALF)
    for s in range(PACK_SUB):
        dst_ref[pl.ds(s, rows, stride=PACK_SUB), :] = words[:, s * LANES:(s + 1) * LANES]


def _unpack_words(words):
    lo = lax.bitcast_convert_type(words << 16, F32)
    hi = lax.bitcast_convert_type(words & HIGH_HALF, F32)
    return lo, hi


def _unpack_rows(src_ref, rows):
    los, his = [], []
    for s in range(PACK_SUB):
        lo, hi = _unpack_words(src_ref[pl.ds(s, rows, stride=PACK_SUB), :])
        los.append(lo)
        his.append(hi)
    return jnp.concatenate(los + his, axis=1)


def _route_kernel(x_ref, g_ref, sh_ref, sc_ref, wr_ref, rb_ref,
                  h_ref, e8_ref, pos8_ref, gate8_ref, cnt_ref,
                  carry, tri_tok, tri_exp):
    i = pl.program_id(0)
    tm = ROUTE_TM
    per_group = N_EXPERTS // N_GROUPS

    @pl.when(i == 0)
    def _():
        carry[...] = jnp.zeros(carry.shape, F32)
        r = lax.broadcasted_iota(I32, (tm, tm), 0)
        c = lax.broadcasted_iota(I32, (tm, tm), 1)
        tri_tok[...] = jnp.where(r <= c, 1.0, 0.0).astype(BF16)
        re = lax.broadcasted_iota(I32, (N_EXPERTS, N_EXPERTS), 0)
        ce = lax.broadcasted_iota(I32, (N_EXPERTS, N_EXPERTS), 1)
        tri_exp[...] = jnp.where(ce < re, 1.0, 0.0).astype(BF16)

    h = _norm_modulate(x_ref[...], g_ref[...], sh_ref[0], sc_ref[0])
    _pack_rows(h, h_ref, tm)
    logits = lax.dot_general(wr_ref[...], h, (((1,), (1,)), ((), ())), precision=HIGHEST,
                             preferred_element_type=F32)
    scores = jax.nn.sigmoid(logits)
    sel = scores + rb_ref[...]

    sel3 = sel.reshape(N_GROUPS, per_group, tm)
    within = lax.broadcasted_iota(I32, sel3.shape, 1)
    max1 = jnp.max(sel3, axis=1, keepdims=True)
    first = jnp.min(jnp.where(sel3 == max1, within, per_group), axis=1, keepdims=True)
    max2 = jnp.max(jnp.where(within == first, -jnp.inf, sel3), axis=1, keepdims=True)
    gscore = max1 + max2

    gidx = lax.broadcasted_iota(I32, gscore.shape, 0)
    grank = jnp.zeros(gscore.shape, I32)
    for g in range(N_GROUPS):
        other = gscore[g:g + 1]
        beats = (other > gscore) | ((other == gscore) & (gidx > g))
        grank = grank + jnp.where(beats, 1, 0)
    keep = grank < TOPK_GROUPS
    masked = jnp.where(keep, sel3, NEG_INF).reshape(N_EXPERTS, tm)

    eidx = lax.broadcasted_iota(I32, (N_EXPERTS, tm), 0)
    erank = jnp.zeros((N_EXPERTS, tm), I32)
    for e in range(N_EXPERTS):
        other = masked[e:e + 1, :]
        beats = (other > masked) | ((other == masked) & (eidx > e))
        erank = erank + jnp.where(beats, 1, 0)
    chosen = erank < TOP_K

    gates = jnp.where(chosen, scores, 0.0)
    gates = gates / jnp.sum(gates, axis=0, keepdims=True) * ROUTED_SCALE

    chosen_b = jnp.where(chosen, 1.0, 0.0).astype(BF16)
    slot = jnp.dot(tri_exp[...], chosen_b, preferred_element_type=F32)
    csum = jnp.dot(chosen_b, tri_tok[...], preferred_element_type=F32)
    pos = carry[:, 0:1] + csum - 1.0
    eidx_f = eidx.astype(F32)
    e_rows, p_rows, g_rows = [], [], []
    for kk in range(TOP_K):
        pick = chosen & (slot == float(kk))
        e_rows.append(jnp.sum(jnp.where(pick, eidx_f, 0.0), axis=0, keepdims=True))
        p_rows.append(jnp.sum(jnp.where(pick, pos, 0.0), axis=0, keepdims=True))
        g_rows.append(jnp.sum(jnp.where(pick, gates, 0.0), axis=0, keepdims=True))
    e8_ref[...] = jnp.concatenate(e_rows, axis=0).astype(I32)
    pos8_ref[...] = jnp.concatenate(p_rows, axis=0).astype(I32)
    gate8_ref[...] = jnp.concatenate(g_rows, axis=0)
    total = carry[...] + csum[:, tm - 1:tm]
    carry[...] = total
    cnt_ref[...] = total


def _route(x2, seq, gain, shift, scale, w_router, router_bias):
    n, d = x2.shape
    tm = ROUTE_TM
    bsz = shift.shape[0]
    return pl.pallas_call(
        _route_kernel,
        grid=(n // tm,),
        in_specs=[
            pl.BlockSpec((tm, d), lambda i: (i, 0)),
            pl.BlockSpec((1, d), lambda i: (0, 0)),
            pl.BlockSpec((1, 1, d), lambda i: (i * tm // seq, 0, 0)),
            pl.BlockSpec((1, 1, d), lambda i: (i * tm // seq, 0, 0)),
            pl.BlockSpec((N_EXPERTS, d), lambda i: (0, 0)),
            pl.BlockSpec((N_EXPERTS, 1), lambda i: (0, 0)),
        ],
        out_specs=[
            pl.BlockSpec((tm * PACK_SUB, LANES), lambda i: (i, 0)),
            pl.BlockSpec((TOP_K, tm), lambda i: (0, i)),
            pl.BlockSpec((TOP_K, tm), lambda i: (0, i)),
            pl.BlockSpec((TOP_K, tm), lambda i: (0, i)),
            pl.BlockSpec((N_EXPERTS, LANES), lambda i: (0, 0)),
        ],
        out_shape=[
            jax.ShapeDtypeStruct((n * PACK_SUB, LANES), U32),
            jax.ShapeDtypeStruct((TOP_K, n), I32),
            jax.ShapeDtypeStruct((TOP_K, n), I32),
            jax.ShapeDtypeStruct((TOP_K, n), F32),
            jax.ShapeDtypeStruct((N_EXPERTS, LANES), F32),
        ],
        scratch_shapes=[
            pltpu.VMEM((N_EXPERTS, LANES), F32),
            pltpu.VMEM((tm, tm), BF16),
            pltpu.VMEM((N_EXPERTS, N_EXPERTS), BF16),
        ],
        compiler_params=_cparams(("arbitrary",)),
        name="route",
    )(x2, gain.reshape(1, d), shift.reshape(bsz, 1, d), scale.reshape(bsz, 1, d),
      w_router.T, router_bias.reshape(N_EXPERTS, 1))


def _tables_kernel(cnt_ref, e8_ref, pos8_ref, dest_ref, bexp_ref, ends_ref, *, nb_pad):
    counts = cnt_ref[...]
    blocks = jnp.ceil(counts / float(MOE_BM))
    re = lax.broadcasted_iota(I32, (N_EXPERTS, N_EXPERTS), 0)
    ce = lax.broadcasted_iota(I32, (N_EXPERTS, N_EXPERTS), 1)
    tri = jnp.where(ce < re, 1.0, 0.0).astype(BF16)
    start_blk = jnp.dot(tri, blocks.astype(BF16), preferred_element_type=F32)
    end_blk = start_blk + blocks
    e8 = e8_ref[...]
    dest = pos8_ref[...].astype(F32)
    for e in range(N_EXPERTS):
        dest = dest + jnp.where(e8 == e, start_blk[e:e + 1, 0:1] * float(MOE_BM), 0.0)
    dest_ref[...] = dest.astype(I32)
    bidx = lax.broadcasted_iota(I32, (N_EXPERTS, nb_pad), 1).astype(F32)
    owner = jnp.sum(jnp.where(end_blk[:, 0:1] <= bidx, 1.0, 0.0), axis=0, keepdims=True)
    bexp_ref[...] = jnp.minimum(owner, float(N_EXPERTS - 1)).astype(I32)
    ends_ref[...] = end_blk.astype(I32)


def _tables(counts, e8, pos8, nb):
    n = e8.shape[1]
    tm = 2048
    nb_pad = -(-nb // LANES) * LANES
    kern = functools.partial(_tables_kernel, nb_pad=nb_pad)
    return pl.pallas_call(
        kern,
        grid=(n // tm,),
        in_specs=[
            pl.BlockSpec((N_EXPERTS, LANES), lambda i: (0, 0)),
            pl.BlockSpec((TOP_K, tm), lambda i: (0, i)),
            pl.BlockSpec((TOP_K, tm), lambda i: (0, i)),
        ],
        out_specs=[
            pl.BlockSpec((TOP_K, tm), lambda i: (0, i)),
            pl.BlockSpec((1, nb_pad), lambda i: (0, 0)),
            pl.BlockSpec((N_EXPERTS, LANES), lambda i: (0, 0)),
        ],
        out_shape=[
            jax.ShapeDtypeStruct((TOP_K, n), I32),
            jax.ShapeDtypeStruct((1, nb_pad), I32),
            jax.ShapeDtypeStruct((N_EXPERTS, LANES), I32),
        ],
        compiler_params=_cparams(("arbitrary",)),
        name="tables",
    )(counts, e8, pos8)


def _dispatch_kernel(ends_ref, tab_hbm, h_ref, xs_hbm, tab_smem, zero_buf,
                     sem_tab, sem_rows, sem_zero, *, n_tiles, n_blocks):
    i = pl.program_id(0)
    slot = i % 2
    tm = DISP_TM
    blk_rows = MOE_BM * PACK_SUB

    def tab_copy(t, sl):
        return pltpu.make_async_copy(tab_hbm.at[t], tab_smem.at[sl], sem_tab.at[sl])

    def zero_copy(e):
        end = ends_ref[e]
        start = jnp.where(e > 0, ends_ref[jnp.maximum(e - 1, 0)], 0)
        row0 = pl.multiple_of(jnp.maximum(end - 1, 0) * blk_rows, PACK_SUB)
        return end > start, pltpu.make_async_copy(
            zero_buf, xs_hbm.at[pl.ds(row0, blk_rows), :], sem_zero)

    @pl.when(i == 0)
    def _():
        tab_copy(0, 0).start()
        zero_buf[...] = jnp.zeros(zero_buf.shape, U32)

        def start_body(e, carry):
            has_rows, cp = zero_copy(e)

            @pl.when(has_rows)
            def _():
                cp.start()
            return carry

        def wait_body(e, carry):
            has_rows, cp = zero_copy(e)

            @pl.when(has_rows)
            def _():
                cp.wait()
            return carry

        def tail_copy(blk):
            row0 = pl.multiple_of(blk * blk_rows, PACK_SUB)
            return pltpu.make_async_copy(zero_buf, xs_hbm.at[pl.ds(row0, blk_rows), :],
                                         sem_zero)

        def tail_start(blk, carry):
            tail_copy(blk).start()
            return carry

        def tail_wait(blk, carry):
            tail_copy(blk).wait()
            return carry

        n_used = ends_ref[N_EXPERTS - 1]
        lax.fori_loop(0, N_EXPERTS, start_body, 0)
        lax.fori_loop(n_used, n_blocks, tail_start, 0)
        lax.fori_loop(0, N_EXPERTS, wait_body, 0)
        lax.fori_loop(n_used, n_blocks, tail_wait, 0)

    tab_copy(i, slot).wait()

    @pl.when(i + 1 < n_tiles)
    def _():
        tab_copy(i + 1, 1 - slot).start()

    def body(r, carry):
        src = pl.multiple_of(r * PACK_SUB, PACK_SUB)
        for kk in range(TOP_K):
            dst = pl.multiple_of(tab_smem[slot, kk * tm + r] * PACK_SUB, PACK_SUB)
            pltpu.make_async_copy(h_ref.at[pl.ds(src, PACK_SUB), :],
                                  xs_hbm.at[pl.ds(dst, PACK_SUB), :], sem_rows).start()
        return carry

    lax.fori_loop(0, tm, body, 0, unroll=2)
    for kk in range(TOP_K):
        pltpu.make_async_copy(h_ref, xs_hbm.at[pl.ds(0, tm * PACK_SUB), :], sem_rows).wait()


def _dispatch(ends, tab, h_packed, nb):
    n_tiles = tab.shape[0]
    kern = functools.partial(_dispatch_kernel, n_tiles=n_tiles, n_blocks=nb)
    grid_spec = pltpu.PrefetchScalarGridSpec(
        num_scalar_prefetch=1,
        grid=(n_tiles,),
        in_specs=[pl.BlockSpec(memory_space=pl.ANY),
                  pl.BlockSpec((DISP_TM * PACK_SUB, LANES), lambda i, ends: (i, 0))],
        out_specs=pl.BlockSpec(memory_space=pl.ANY),
        scratch_shapes=[
            pltpu.SMEM((2, TOP_K * DISP_TM), I32),
            pltpu.VMEM((MOE_BM * PACK_SUB, LANES), U32),
            pltpu.SemaphoreType.DMA((2,)),
            pltpu.SemaphoreType.DMA,
            pltpu.SemaphoreType.DMA,
        ],
    )
    return pl.pallas_call(
        kern,
        grid_spec=grid_spec,
        out_shape=jax.ShapeDtypeStruct((nb * MOE_BM * PACK_SUB, LANES), U32),
        compiler_params=_cparams(("arbitrary",)),
        name="dispatch",
    )(ends, tab, h_packed)


def _experts_kernel(bexp_ref, nused_ref, xs_ref, wu_ref, wd_ref, y_ref, wu_bf, wd_bf):
    b = pl.program_id(0)
    nu = nused_ref[0]
    bm = MOE_BM

    prev_expert = bexp_ref[jnp.maximum(b - 1, 0)]

    @pl.when((b == 0) | (bexp_ref[b] != prev_expert))
    def _():
        wu_bf[...] = wu_ref[...].astype(BF16)
        wd_bf[...] = wd_ref[...].astype(BF16)

    @pl.when(b < nu)
    def _():
        x = _unpack_rows(xs_ref, bm).astype(BF16)
        up = jnp.dot(x, wu_bf[...], preferred_element_type=F32)
        act = _silu(up[:, :EXPERT_DIM]) * up[:, EXPERT_DIM:]
        y = jnp.dot(act.astype(BF16), wd_bf[...], preferred_element_type=F32)
        _pack_rows(y, y_ref, bm)

    @pl.when(b >= nu)
    def _():
        y_ref[...] = jnp.zeros(y_ref.shape, U32)


def _experts(bexp, nused, xs, w_up_all, w_down_all, layer):
    bm = MOE_BM
    nb = xs.shape[0] // (bm * PACK_SUB)
    d = w_up_all.shape[2]
    grid_spec = pltpu.PrefetchScalarGridSpec(
        num_scalar_prefetch=2,
        grid=(nb,),
        in_specs=[
            pl.BlockSpec((bm * PACK_SUB, LANES),
                         lambda b, be, nu: (jnp.minimum(b, nu[0] - 1), 0)),
            pl.BlockSpec((None, None, d, 2 * EXPERT_DIM),
                         lambda b, be, nu: (layer, be[b], 0, 0)),
            pl.BlockSpec((None, None, EXPERT_DIM, d),
                         lambda b, be, nu: (layer, be[b], 0, 0)),
        ],
        out_specs=pl.BlockSpec((bm * PACK_SUB, LANES), lambda b, be, nu: (b, 0)),
        scratch_shapes=[
            pltpu.VMEM((d, 2 * EXPERT_DIM), BF16),
            pltpu.VMEM((EXPERT_DIM, d), BF16),
        ],
    )
    return pl.pallas_call(
        _experts_kernel,
        grid_spec=grid_spec,
        out_shape=jax.ShapeDtypeStruct((nb * bm * PACK_SUB, LANES), U32),
        compiler_params=_cparams(("arbitrary",)),
        name="experts",
    )(bexp, nused, xs, w_up_all, w_down_all)


def _combine_kernel(tab_hbm, y_hbm, gate_ref, h_ref, x_ref, gf_ref, wsu_ref, wsd_ref, out_ref,
                    ybuf, tab_smem, shared_scr, sem_tab, sem_rows, *, n_tiles):
    i = pl.program_id(0)
    slot = i % 2
    tm = COMB_TM

    def tab_copy(t, sl):
        return pltpu.make_async_copy(tab_hbm.at[t], tab_smem.at[sl], sem_tab.at[sl])

    def issue_rows(sl):
        for kk in range(TOP_K):
            def body(r, carry, kk=kk):
                src = pl.multiple_of(tab_smem[sl, kk * tm + r] * PACK_SUB, PACK_SUB)
                dst = pl.multiple_of(r * PACK_SUB, PACK_SUB)
                pltpu.make_async_copy(y_hbm.at[pl.ds(src, PACK_SUB), :],
                                      ybuf.at[sl, kk, pl.ds(dst, PACK_SUB), :],
                                      sem_rows.at[sl]).start()
                return carry
            lax.fori_loop(0, tm, body, 0, unroll=8)

    @pl.when(i == 0)
    def _():
        tab_copy(0, 0).start()
        tab_copy(0, 0).wait()
        issue_rows(0)
        if n_tiles > 1:
            tab_copy(1, 1).start()

    @pl.when(i + 1 < n_tiles)
    def _():
        tab_copy(i + 1, 1 - slot).wait()
        issue_rows(1 - slot)

    @pl.when(i + 2 < n_tiles)
    def _():
        tab_copy(i + 2, slot).start()

    hb = _unpack_rows(h_ref, tm).astype(BF16)
    up = jnp.dot(hb, wsu_ref[...], preferred_element_type=F32)
    half = wsd_ref.shape[0]
    act = _silu(up[:, :half]) * up[:, half:]
    shared_scr[...] = jnp.dot(act.astype(BF16), wsd_ref[...], preferred_element_type=F32)

    for kk in range(TOP_K):
        pltpu.make_async_copy(y_hbm.at[pl.ds(0, tm * PACK_SUB), :], ybuf.at[slot, kk],
                              sem_rows.at[slot]).wait()
    gate = gate_ref[...]
    gates = [jnp.broadcast_to(gate[:, kk:kk + 1], (tm, LANES)) for kk in range(TOP_K)]
    d_half = PACK_SUB * LANES
    for s in range(PACK_SUB):
        acc_lo = jnp.zeros((tm, LANES), F32)
        acc_hi = jnp.zeros((tm, LANES), F32)
        for kk in range(TOP_K):
            lo, hi = _unpack_words(ybuf[slot, kk, pl.ds(s, tm, stride=PACK_SUB), :])
            acc_lo = acc_lo + gates[kk] * lo
            acc_hi = acc_hi + gates[kk] * hi
        for acc, c0 in ((acc_lo, s * LANES), (acc_hi, d_half + s * LANES)):
            cs = slice(c0, c0 + LANES)
            out_ref[:, cs] = x_ref[:, cs] + gf_ref[0, :, cs] * (acc + shared_scr[:, cs])


def _combine(tab, y, gate_t, h, x2, g_f, w_su, w_sd, seq):
    n, d = x2.shape
    tm = COMB_TM
    n_tiles = n // tm
    bsz = g_f.shape[0]
    kern = functools.partial(_combine_kernel, n_tiles=n_tiles)
    return pl.pallas_call(
        kern,
        grid=(n_tiles,),
        in_specs=[
            pl.BlockSpec(memory_space=pl.ANY),
            pl.BlockSpec(memory_space=pl.ANY),
            pl.BlockSpec((tm, TOP_K), lambda i: (i, 0)),
            pl.BlockSpec((tm * PACK_SUB, LANES), lambda i: (i, 0)),
            pl.BlockSpec((tm, d), lambda i: (i, 0)),
            pl.BlockSpec((1, 1, d), lambda i: (i * tm // seq, 0, 0)),
            pl.BlockSpec(w_su.shape, lambda i: (0, 0)),
            pl.BlockSpec(w_sd.shape, lambda i: (0, 0)),
        ],
        out_specs=pl.BlockSpec((tm, d), lambda i: (i, 0)),
        out_shape=jax.ShapeDtypeStruct((n, d), F32),
        scratch_shapes=[
            pltpu.VMEM((2, TOP_K, tm * PACK_SUB, LANES), U32),
            pltpu.SMEM((2, TOP_K * tm), I32),
            pltpu.VMEM((tm, d), F32),
            pltpu.SemaphoreType.DMA((2,)),
            pltpu.SemaphoreType.DMA((2,)),
        ],
        compiler_params=_cparams(("arbitrary",)),
        name="combine",
    )(tab, y, gate_t, h, x2, g_f.reshape(bsz, 1, d), w_su, w_sd)


def _tile_table(dest8, tm):
    n = dest8.shape[1]
    return dest8.reshape(TOP_K, n // tm, tm).transpose(1, 0, 2).reshape(n // tm, TOP_K * tm)


def _moe(x2, seq, gain, shift, scale, g_f, w_router, router_bias, w_up_all, w_down_all, layer,
         w_su, w_sd):
    n, d = x2.shape
    assert d == 2 * PACK_SUB * LANES
    h, e8, pos8, gate8, counts = _route(x2, seq, gain, shift, scale, w_router, router_bias)
    nb = -(-(n * TOP_K + N_EXPERTS * (MOE_BM - 1)) // MOE_BM)
    dest8, bexp, ends = _tables(counts, e8, pos8, nb)
    ends = ends[:, 0]
    xs = _dispatch(ends, _tile_table(dest8, DISP_TM), h, nb)
    y = _experts(bexp[0, :nb], ends[N_EXPERTS - 1:], xs, w_up_all, w_down_all, layer)
    return _combine(_tile_table(dest8, COMB_TM), y, gate8.T, h, x2, g_f, w_su.astype(BF16),
                    w_sd.astype(BF16), seq)


def _alibi_slopes(nh):
    return jnp.asarray(2.0 ** (-8.0 * np.arange(1, nh + 1) / nh), dtype=F32)


def _even_weights(w_in, qk_norm):
    a_w = A_HEADS * HEAD_DIM
    b_w = B_HEADS * HEAD_DIM
    kv_w = B_KV_HEADS * HEAD_DIM
    cuts = np.cumsum([a_w, a_w, a_w, b_w, kv_w])
    qa, ka, va, qb, kb, vb = jnp.split(w_in, cuts, axis=1)

    def dup(a):
        return jnp.concatenate([a[:, hh * HEAD_DIM:(hh + 1) * HEAD_DIM]
                                for hh in range(B_KV_HEADS) for _ in range(2)], axis=1)

    w = jnp.concatenate([qa, ka, qb, dup(kb), va, dup(vb)], axis=1).astype(BF16)
    q_scale = HEAD_DIM ** -0.5
    gains = jnp.concatenate([
        jnp.tile(qk_norm[0] * q_scale, A_HEADS), jnp.tile(qk_norm[1], A_HEADS),
        jnp.tile(qk_norm[2] * q_scale, B_HEADS), jnp.tile(qk_norm[3], 2 * B_KV_HEADS),
        jnp.ones((a_w + 2 * kv_w,), F32)])
    n_norm = 2 * a_w + b_w + 2 * kv_w
    return w, gains, n_norm


def kernel(x, c, norm_mix, norm_ffn, w_ada, b_ada, ab_w_in, ab_qk_norm, ab_sinks, ab_w_out,
           c_w_in, c_qk_norm, c_lambda, c_subln, c_w_out, w_router, router_bias,
           experts_up, experts_down, shared_up, shared_down):
    bsz, seq, d = x.shape
    depth = w_ada.shape[0]
    n = bsz * seq
    ada = _ada(c, w_ada, b_ada)
    x2 = x.reshape(n, d)
    for i in range(depth):
        sh_m, sc_m, g_m, sh_f, sc_f, g_f = jnp.split(ada[i], 6, axis=-1)
        j = i // 2
        if i % 2 == 0:
            w, gains, n_norm = _even_weights(ab_w_in[j], ab_qk_norm[j])
            qkv = _proj(x2, seq, norm_mix[i], sh_m, sc_m, w, gains, n_norm)
            pa = A_HEADS // 2
            pb = B_HEADS // 2
            q_a, k_a, q_b, k_b = 0, pa, 2 * pa, 2 * pa + pb
            v_a = k_b + B_KV_HEADS
            v_b = v_a + pa
            o_a = _attn_a(qkv, bsz, seq, q_a, k_a, v_a, _alibi_slopes(A_HEADS))
            o_b = _attn_b(qkv, bsz, seq, q_b, k_b, v_b, _alibi_slopes(B_HEADS),
                          ab_sinks[j].astype(F32))
            x2 = _outproj(o_a.reshape(n, -1), 0, o_b.reshape(n, -1), 0,
                          ab_w_out[j].astype(BF16), x2, g_m, seq)
        else:
            q_scale = HEAD_DIM ** -0.5 * LOG2E
            gains = jnp.concatenate([
                jnp.tile(c_qk_norm[j, 0] * q_scale, 2 * C_HEADS),
                jnp.tile(c_qk_norm[j, 1], 2 * C_HEADS),
                jnp.ones((2 * C_HEADS * HEAD_DIM,), F32)])
            qkv = _proj(x2, seq, norm_mix[i], sh_m, sc_m, c_w_in[j].astype(BF16), gains,
                        4 * C_HEADS * HEAD_DIM)
            lam_init = 0.8 - 0.6 * math.exp(-0.3 * i)
            o_c = _attn_c(qkv, bsz, seq, _alibi_slopes(C_HEADS), c_qk_norm[j],
                          c_lambda[j].astype(F32),
                          c_subln[j].astype(F32), lam_init).reshape(n, -1)
            x2 = _outproj(o_c, 0, o_c, 1, c_w_out[j].astype(BF16), x2, g_m, seq)
        x2 = _moe(x2, seq, norm_ffn[i], sh_f, sc_f, g_f, w_router[i], router_bias[i],
                  experts_up, experts_down, i, shared_up[i], shared_down[i])
    return x2.reshape(bsz, seq, d)
```

```python
import functools
import math

import jax
import jax.numpy as jnp
import numpy as np
from jax import lax
from jax.experimental import pallas as pl
from jax.experimental.pallas import tpu as pltpu

F32 = jnp.float32
BF16 = jnp.bfloat16
I32 = jnp.int32
HIGHEST = lax.Precision.HIGHEST

HEAD_DIM = 64
A_HEADS = 16
DILATIONS = (1, 4, 16)
A_MAX_DIST = 128
B_HEADS = 16
B_KV_HEADS = 2
B_MAX_DIST = 127
C_HEADS = 16
N_EXPERTS = 64
N_GROUPS = 8
TOPK_GROUPS = 4
TOP_K = 8
EXPERT_DIM = 512
ROUTED_SCALE = 2.5
RMS_EPS = 1e-6
NEG_INF = -1e30
LOG2E = math.log2(math.e)
SCORE_BOUND_MARGIN = 1.02
MAX_BOUNDED_EXPONENT = 100.0

LANES = 128
VMEM_LIMIT = 56 * 1024 * 1024

PROJ_TM = 1024
PROJ_TN = 512
ATT_TILE = 2048
BAND = 128
DIFF_TQ = 512
DIFF_TK = 512
ROUTE_TM = 512
MOE_BM = 512
COMB_TM = 256
DISP_TM = 512


def _cparams(sem):
    return pltpu.CompilerParams(dimension_semantics=sem, vmem_limit_bytes=VMEM_LIMIT)


def _silu(x):
    return x * jax.nn.sigmoid(x)


def _ada_kernel(c_ref, w_ref, b_ref, o_ref):
    cond = _silu(c_ref[...])
    o_ref[...] = jnp.dot(cond, w_ref[...], precision=HIGHEST,
                         preferred_element_type=F32) + b_ref[...]


def _ada(c, w_ada, b_ada):
    depth, d, d6 = w_ada.shape
    b = c.shape[0]
    bp = 8
    cp = jnp.zeros((bp, d), F32).at[:b].set(c)
    tn = 1024
    out = pl.pallas_call(
        _ada_kernel,
        grid=(depth, d6 // tn),
        in_specs=[
            pl.BlockSpec((bp, d), lambda i, j: (0, 0)),
            pl.BlockSpec((None, d, tn), lambda i, j: (i, 0, j)),
            pl.BlockSpec((None, 1, tn), lambda i, j: (i, 0, j)),
        ],
        out_specs=pl.BlockSpec((None, bp, tn), lambda i, j: (i, 0, j)),
        out_shape=jax.ShapeDtypeStruct((depth, bp, d6), F32),
        compiler_params=_cparams(("arbitrary", "arbitrary")),
        name="ada",
    )(cp, w_ada, b_ada.reshape(depth, 1, d6))
    return out[:, :b]


def _norm_modulate(x, g, sh, sc):
    ms = jnp.mean(x * x, axis=-1, keepdims=True)
    y = x * lax.rsqrt(ms + RMS_EPS) * g
    return y * (1.0 + sc) + sh


def _proj_kernel(x_ref, g_ref, sh_ref, sc_ref, w_ref, hg_ref, pm_ref, o_ref, h_scr, *,
                 n_norm_cols, tn):
    j = pl.program_id(1)

    @pl.when(j == 0)
    def _():
        h = _norm_modulate(x_ref[...], g_ref[...], sh_ref[0], sc_ref[0])
        h_scr[...] = h.astype(BF16)

    acc = jnp.dot(h_scr[...], w_ref[...], preferred_element_type=F32)
    for cb in range(tn // LANES):
        a = acc[:, cb * LANES:(cb + 1) * LANES]
        is_norm = j * tn + cb * LANES < n_norm_cols

        @pl.when(is_norm)
        def _():
            sq = a * a
            hi = sq.astype(BF16)
            lo = (sq - hi.astype(F32)).astype(BF16)
            ms = (jnp.dot(hi, pm_ref[...], preferred_element_type=F32)
                  + jnp.dot(lo, pm_ref[...], preferred_element_type=F32))
            y = a * lax.rsqrt(ms + RMS_EPS) * hg_ref[:, cb * LANES:(cb + 1) * LANES]
            o_ref[cb] = y.astype(BF16)

        @pl.when(jnp.logical_not(is_norm))
        def _():
            o_ref[cb] = a.astype(BF16)


def _head_mean_matrix():
    r = np.arange(LANES)
    pm = (r[:, None] // HEAD_DIM == r[None, :] // HEAD_DIM).astype(np.float32) / HEAD_DIM
    return jnp.asarray(pm, BF16)


def _proj(x2, seq, gain, shift, scale, w, head_gain, n_norm_cols):
    n, d = x2.shape
    cols = w.shape[1]
    tm, tn = PROJ_TM, PROJ_TN
    bsz = shift.shape[0]
    kern = functools.partial(_proj_kernel, n_norm_cols=n_norm_cols, tn=tn)
    return pl.pallas_call(
        kern,
        grid=(n // tm, cols // tn),
        in_specs=[
            pl.BlockSpec((tm, d), lambda i, j: (i, 0)),
            pl.BlockSpec((1, d), lambda i, j: (0, 0)),
            pl.BlockSpec((1, 1, d), lambda i, j: (i * tm // seq, 0, 0)),
            pl.BlockSpec((1, 1, d), lambda i, j: (i * tm // seq, 0, 0)),
            pl.BlockSpec((d, tn), lambda i, j: (0, j)),
            pl.BlockSpec((1, tn), lambda i, j: (0, j)),
            pl.BlockSpec((LANES, LANES), lambda i, j: (0, 0)),
        ],
        out_specs=pl.BlockSpec((tn // LANES, tm, LANES), lambda i, j: (j, i, 0)),
        out_shape=jax.ShapeDtypeStruct((cols // LANES, n, LANES), BF16),
        scratch_shapes=[pltpu.VMEM((tm, d), BF16)],
        compiler_params=_cparams(("arbitrary", "arbitrary")),
        name="proj",
    )(x2, gain.reshape(1, d), shift.reshape(bsz, 1, d), scale.reshape(bsz, 1, d), w,
      head_gain.reshape(1, cols), _head_mean_matrix())


def _band_block(qz, kk, v_aug, bias, sink=None):
    s = lax.dot_general(qz, kk, (((1,), (1,)), ((), ())), preferred_element_type=F32) + bias
    m = jnp.max(s, axis=-1, keepdims=True)
    if sink is not None:
        m = jnp.maximum(m, sink)
    p = jnp.exp2(s - m)
    res = jnp.dot(p.astype(BF16), v_aug, preferred_element_type=F32)
    return res, m


def _band_masks(max_dist):
    qi = lax.broadcasted_iota(I32, (BAND, 2 * BAND), 0)
    kj = lax.broadcasted_iota(I32, (BAND, 2 * BAND), 1)
    dist = BAND + qi - kj
    valid = (dist >= 0) & (dist <= max_dist)
    valid_first = valid & (kj >= BAND)
    return dist.astype(F32), valid, valid_first


def _attn_a_kernel(slopes_ref,
                   q1, q4, q16,
                   k1c, k1p, k4c, k4p, k16c, k16p,
                   v1c, v1p, v4c, v4p, v16c, v16p,
                   o_ref, kcat, vcat, st):
    p = pl.program_id(1)
    n = pl.program_id(2)
    lane = lax.broadcasted_iota(I32, (BAND, LANES), 1)
    distf, valid, valid_first = _band_masks(A_MAX_DIST)
    seq_start = n == 0
    branches = ((1, q1, k1c, k1p, v1c, v1p), (4, q4, k4c, k4p, v4c, v4p),
                (16, q16, k16c, k16p, v16c, v16p))
    for bi, (d, qv, kc, kp, vc, vp) in enumerate(branches):
        n_l = ATT_TILE // BAND // d
        rows = n_l * BAND
        biases = []
        for h in range(2):
            slope = slopes_ref[2 * p + h] * (float(d) * LOG2E)
            b_full = jnp.where(valid, -slope * distf, NEG_INF)
            b_first = jnp.where(valid_first, -slope * distf, NEG_INF)
            biases.append((b_full, jnp.where(seq_start, b_first, b_full)))
        for r in range(d):
            cs = slice(r * LANES, (r + 1) * LANES)
            kcat[0:BAND, :] = kp[:, cs]
            kcat[BAND:BAND + rows, :] = kc[:, cs]
            vprev = vp[:, cs]
            vcur = vc[:, cs]
            for h in range(2):
                vcat[h, 0:BAND, :] = jnp.where(lane // HEAD_DIM == h, vprev, 1.0).astype(BF16)
                lane_k = lax.broadcasted_iota(I32, (rows, LANES), 1)
                vcat[h, BAND:BAND + rows, :] = jnp.where(
                    lane_k // HEAD_DIM == h, vcur, 1.0).astype(BF16)
            for h in range(2):
                m_lane = LANES - 1 if h == 0 else 0
                for jb in range(n_l):
                    qb = qv[jb * BAND:(jb + 1) * BAND, cs]
                    qz = jnp.where(lane // HEAD_DIM == h, qb, 0.0).astype(BF16)
                    kk = kcat[jb * BAND:jb * BAND + 2 * BAND, :]
                    vv = vcat[h, jb * BAND:jb * BAND + 2 * BAND, :]
                    bias = biases[h][1] if jb == 0 else biases[h][0]
                    res, m = _band_block(qz, kk, vv, bias)
                    res = jnp.where(lane == m_lane, m, res)
                    if d == 1:
                        st[bi, h, jb * BAND:(jb + 1) * BAND, :] = res
                    else:
                        st[bi, h, pl.ds(jb * BAND * d + r, BAND, stride=d), :] = res
    for ib in range(ATT_TILE // BAND):
        rs = slice(ib * BAND, (ib + 1) * BAND)
        for h in range(2):
            m_lane = LANES - 1 if h == 0 else 0
            l_lane = HEAD_DIM if h == 0 else 1
            hs = slice(h * HEAD_DIM, (h + 1) * HEAD_DIM)
            parts = [st[bi, h, rs, :] for bi in range(3)]
            ms = [x[:, m_lane:m_lane + 1] for x in parts]
            m_all = jnp.maximum(jnp.maximum(ms[0], ms[1]), ms[2])
            num = sum(jnp.exp2(mm - m_all) * x for mm, x in zip(ms, parts))
            out = num / num[:, l_lane:l_lane + 1]
            o_ref[rs, hs] = out[:, hs].astype(BF16)


def _attn_a(qkv, bsz, seq, q_cb, k_cb, v_cb, slopes):
    cb = qkv.shape[0]
    views = {d: qkv.reshape(cb, bsz, seq // d, d * LANES) for d in DILATIONS}
    n_pairs = A_HEADS // 2
    tile = ATT_TILE
    in_specs = [pl.BlockSpec(memory_space=pltpu.SMEM)]
    args = [slopes]
    for d in DILATIONS:
        in_specs.append(pl.BlockSpec((None, None, tile // d, d * LANES),
                                     lambda b, p, n: (q_cb + p, b, n, 0)))
        args.append(views[d])
    for base in (k_cb, v_cb):
        for d in DILATIONS:
            per = tile // d // BAND
            in_specs.append(pl.BlockSpec((None, None, tile // d, d * LANES),
                                         lambda b, p, n, base=base: (base + p, b, n, 0)))
            in_specs.append(pl.BlockSpec(
                (None, None, BAND, d * LANES),
                lambda b, p, n, base=base, per=per: (base + p, b, jnp.maximum(n * per - 1, 0), 0)))
            args += [views[d], views[d]]
    return pl.pallas_call(
        _attn_a_kernel,
        grid=(bsz, n_pairs, seq // tile),
        in_specs=in_specs,
        out_specs=pl.BlockSpec((None, tile, LANES), lambda b, p, n: (b, n, p)),
        out_shape=jax.ShapeDtypeStruct((bsz, seq, A_HEADS * HEAD_DIM), BF16),
        scratch_shapes=[
            pltpu.VMEM((BAND + tile, LANES), BF16),
            pltpu.VMEM((2, BAND + tile, LANES), BF16),
            pltpu.VMEM((3, 2, tile, LANES), F32),
        ],
        compiler_params=_cparams(("arbitrary", "arbitrary", "arbitrary")),
        name="attn_dilated",
    )(*args)


def _attn_b_kernel(slopes_ref, sinks_ref, q_ref, kc, kp, vc, vp, o_ref, kcat, vcat):
    p = pl.program_id(1)
    n = pl.program_id(2)
    lane = lax.broadcasted_iota(I32, (BAND, LANES), 1)
    lane_k = lax.broadcasted_iota(I32, (ATT_TILE, LANES), 1)
    distf, valid, valid_first = _band_masks(B_MAX_DIST)
    seq_start = n == 0
    n_l = ATT_TILE // BAND
    kcat[0:BAND, :] = kp[...]
    kcat[BAND:, :] = kc[...]
    vprev = vp[...]
    vcur = vc[...]
    for h in range(2):
        vcat[h, 0:BAND, :] = jnp.where(lane // HEAD_DIM == h, vprev, 1.0).astype(BF16)
        vcat[h, BAND:, :] = jnp.where(lane_k // HEAD_DIM == h, vcur, 1.0).astype(BF16)
    for h in range(2):
        slope = slopes_ref[2 * p + h] * LOG2E
        sink = sinks_ref[2 * p + h] * LOG2E
        b_full = jnp.where(valid, -slope * distf, NEG_INF)
        b_first = jnp.where(seq_start, jnp.where(valid_first, -slope * distf, NEG_INF), b_full)
        l_lane = HEAD_DIM if h == 0 else 0
        for jb in range(n_l):
            qb = q_ref[jb * BAND:(jb + 1) * BAND, :]
            qz = jnp.where(lane // HEAD_DIM == h, qb, 0.0).astype(BF16)
            kk = kcat[jb * BAND:jb * BAND + 2 * BAND, :]
            vv = vcat[h, jb * BAND:jb * BAND + 2 * BAND, :]
            res, m = _band_block(qz, kk, vv, b_first if jb == 0 else b_full, sink=sink)
            denom = res[:, l_lane:l_lane + 1] + jnp.exp2(sink - m)
            hs = slice(h * HEAD_DIM, (h + 1) * HEAD_DIM)
            o_ref[jb * BAND:(jb + 1) * BAND, hs] = (res / denom)[:, hs].astype(BF16)


def _attn_b(qkv, bsz, seq, q_cb, k_cb, v_cb, slopes, sinks):
    cb = qkv.shape[0]
    view = qkv.reshape(cb, bsz, seq, LANES)
    n_pairs = B_HEADS // 2
    pairs_per_kv = n_pairs // B_KV_HEADS
    tile = ATT_TILE
    per = tile // BAND

    def cur(base):
        return pl.BlockSpec((None, None, tile, LANES),
                            lambda b, p, n: (base + p // pairs_per_kv, b, n, 0))

    def prev(base):
        return pl.BlockSpec(
            (None, None, BAND, LANES),
            lambda b, p, n: (base + p // pairs_per_kv, b, jnp.maximum(n * per - 1, 0), 0))

    return pl.pallas_call(
        _attn_b_kernel,
        grid=(bsz, n_pairs, seq // tile),
        in_specs=[
            pl.BlockSpec(memory_space=pltpu.SMEM),
            pl.BlockSpec(memory_space=pltpu.SMEM),
            pl.BlockSpec((None, None, tile, LANES), lambda b, p, n: (q_cb + p, b, n, 0)),
            cur(k_cb), prev(k_cb), cur(v_cb), prev(v_cb),
        ],
        out_specs=pl.BlockSpec((None, tile, LANES), lambda b, p, n: (b, n, p)),
        out_shape=jax.ShapeDtypeStruct((bsz, seq, B_HEADS * HEAD_DIM), BF16),
        scratch_shapes=[
            pltpu.VMEM((BAND + tile, LANES), BF16),
            pltpu.VMEM((2, BAND + tile, LANES), BF16),
        ],
        compiler_params=_cparams(("arbitrary", "arbitrary", "arbitrary")),
        name="attn_swa",
    )(slopes, sinks, view, view, view, view, view)


def _attn_c_kernel(qi_tab, ki_tab, bounded_ref, consts_ref, lam_ref, subln_ref,
                   q_ref, k_ref, v_ref, o_ref, m_scr, l_scr, acc_scr, bias_scr, *, lam_init):
    h = pl.program_id(1)
    step = pl.program_id(2)
    qi = qi_tab[step]
    ki = ki_tab[step]
    tq, tk = DIFF_TQ, DIFF_TK
    slope2 = consts_ref[h] * LOG2E
    bound2 = consts_ref[C_HEADS]
    bounded = bounded_ref[0] == 1
    tile_off = slope2 * ((ki - qi) * tk).astype(F32)

    @pl.when(step == 0)
    def _():
        rel = (lax.broadcasted_iota(I32, (tq, tk), 1)
               - lax.broadcasted_iota(I32, (tq, tk), 0)).astype(F32)
        bias = slope2 * rel
        bias_scr[0] = bias
        bias_scr[1] = jnp.where(rel <= 0.0, bias, NEG_INF)

    @pl.when(ki == 0)
    def _():
        m_scr[...] = jnp.full(m_scr.shape, NEG_INF, F32)
        l_scr[...] = jnp.zeros(l_scr.shape, F32)
        acc_scr[...] = jnp.zeros(acc_scr.shape, F32)

    def scores(a, q, k, lane):
        qz = jnp.where(lane // HEAD_DIM == a, q, 0.0).astype(BF16)
        return lax.dot_general(qz, k, (((1,), (1,)), ((), ())), preferred_element_type=F32)

    def update_online(bias_idx):
        q = q_ref[...]
        k = k_ref[...]
        v = v_ref[...]
        lane = lax.broadcasted_iota(I32, (tq, LANES), 1)
        for a in range(2):
            s = scores(a, q, k, lane) + bias_scr[bias_idx]
            m_prev = m_scr[a]
            m_new = jnp.maximum(m_prev, jnp.max(s, axis=-1, keepdims=True) + tile_off)
            alpha = jnp.exp2(m_prev - m_new)
            pr = jnp.exp2(s - (m_new - tile_off))
            l_scr[a] = alpha * l_scr[a] + jnp.sum(pr, axis=-1, keepdims=True)
            acc_scr[a, :, 0:LANES] = alpha * acc_scr[a, :, 0:LANES] + jnp.dot(
                pr.astype(BF16), v, preferred_element_type=F32)
            m_scr[a] = m_new

    def update_bounded(bias_idx):
        q = q_ref[...]
        k = k_ref[...]
        v = v_ref[...]
        lane = lax.broadcasted_iota(I32, (tq, LANES), 1)
        bias = bias_scr[bias_idx] + (tile_off - bound2)
        v_aug = jnp.concatenate([v, jnp.ones_like(v)], axis=1)
        for a in range(2):
            pr = jnp.exp2(scores(a, q, k, lane) + bias).astype(BF16)
            acc_scr[a] = acc_scr[a] + jnp.dot(pr, v_aug, preferred_element_type=F32)

    def finish(o1, o2):
        lp = lam_ref[...]
        lam = (jnp.exp(jnp.sum(lp[0:1] * lp[1:2], axis=-1, keepdims=True))
               - jnp.exp(jnp.sum(lp[2:3] * lp[3:4], axis=-1, keepdims=True)) + lam_init)
        o = o1 - lam * o2
        ms = jnp.mean(o * o, axis=-1, keepdims=True)
        o = o * lax.rsqrt(ms + RMS_EPS) * subln_ref[...] * (1.0 - lam_init)
        o_ref[...] = o.astype(BF16)

    @pl.when(bounded & (ki < qi))
    def _():
        update_bounded(0)

    @pl.when(bounded & (ki == qi))
    def _():
        update_bounded(1)
        finish(acc_scr[0, :, 0:LANES] / acc_scr[0, :, LANES:],
               acc_scr[1, :, 0:LANES] / acc_scr[1, :, LANES:])

    @pl.when(jnp.logical_not(bounded) & (ki < qi))
    def _():
        update_online(0)

    @pl.when(jnp.logical_not(bounded) & (ki == qi))
    def _():
        update_online(1)
        finish(acc_scr[0, :, 0:LANES] / l_scr[0], acc_scr[1, :, 0:LANES] / l_scr[1])


def _attn_c(qkv, bsz, seq, slopes, qk_gains, lam_params, subln, lam_init):
    cb = qkv.shape[0]
    view = qkv.reshape(cb, bsz, seq, LANES)
    tq, tk = DIFF_TQ, DIFF_TK
    assert tq == tk
    nq = seq // tq
    pairs = [(i, j) for i in range(nq) for j in range(i + 1)]
    qi_tab = jnp.asarray([p[0] for p in pairs], I32)
    ki_tab = jnp.asarray([p[1] for p in pairs], I32)
    g_q = jnp.max(jnp.abs(qk_gains[0].astype(F32)))
    g_k = jnp.max(jnp.abs(qk_gains[1].astype(F32)))
    bound2 = SCORE_BOUND_MARGIN * HEAD_DIM ** 0.5 * g_q * g_k * LOG2E
    bounded = (2.0 * bound2 < MAX_BOUNDED_EXPONENT).astype(I32).reshape(1)
    consts = jnp.concatenate([slopes, bound2.reshape(1)])
    kern = functools.partial(_attn_c_kernel, lam_init=lam_init)

    def im(fn):
        return lambda b, h, s, qt, kt, bd: fn(b, h, s, qt, kt)

    grid_spec = pltpu.PrefetchScalarGridSpec(
        num_scalar_prefetch=3,
        grid=(bsz, C_HEADS, len(pairs)),
        in_specs=[
            pl.BlockSpec(memory_space=pltpu.SMEM),
            pl.BlockSpec((4, HEAD_DIM), im(lambda b, h, s, qt, kt: (0, 0))),
            pl.BlockSpec((1, 2 * HEAD_DIM), im(lambda b, h, s, qt, kt: (0, 0))),
            pl.BlockSpec((None, None, tq, LANES), im(lambda b, h, s, qt, kt: (h, b, qt[s], 0))),
            pl.BlockSpec((None, None, tk, LANES),
                         im(lambda b, h, s, qt, kt: (C_HEADS + h, b, kt[s], 0))),
            pl.BlockSpec((None, None, tk, LANES),
                         im(lambda b, h, s, qt, kt: (2 * C_HEADS + h, b, kt[s], 0))),
        ],
        out_specs=pl.BlockSpec((None, tq, LANES), im(lambda b, h, s, qt, kt: (b, qt[s], h))),
        scratch_shapes=[
            pltpu.VMEM((2, tq, 1), F32),
            pltpu.VMEM((2, tq, 1), F32),
            pltpu.VMEM((2, tq, 2 * LANES), F32),
            pltpu.VMEM((2, tq, tk), F32),
        ],
    )
    return pl.pallas_call(
        kern,
        grid_spec=grid_spec,
        out_shape=jax.ShapeDtypeStruct((bsz, seq, C_HEADS * 2 * HEAD_DIM), BF16),
        compiler_params=_cparams(("arbitrary", "arbitrary", "arbitrary")),
        name="attn_diff",
    )(qi_tab, ki_tab, bounded, consts, lam_params, subln.reshape(1, 2 * HEAD_DIM),
      view, view, view)


def _outproj_kernel(o1_ref, o2_ref, w1_ref, w2_ref, x_ref, g_ref, out_ref):
    acc = (jnp.dot(o1_ref[...], w1_ref[...], preferred_element_type=F32)
           + jnp.dot(o2_ref[...], w2_ref[...], preferred_element_type=F32))
    out_ref[...] = x_ref[...] + g_ref[0] * acc


def _outproj(o1, cb1, o2, cb2, w, x2, gate, seq):
    n, d = x2.shape
    half = w.shape[0] // 2
    tm, tn = PROJ_TM, PROJ_TN
    bsz = gate.shape[0]
    return pl.pallas_call(
        _outproj_kernel,
        grid=(n // tm, d // tn),
        in_specs=[
            pl.BlockSpec((tm, half), lambda i, j: (i, cb1)),
            pl.BlockSpec((tm, half), lambda i, j: (i, cb2)),
            pl.BlockSpec((half, tn), lambda i, j: (0, j)),
            pl.BlockSpec((half, tn), lambda i, j: (1, j)),
            pl.BlockSpec((tm, tn), lambda i, j: (i, j)),
            pl.BlockSpec((1, 1, tn), lambda i, j: (i * tm // seq, 0, j)),
        ],
        out_specs=pl.BlockSpec((tm, tn), lambda i, j: (i, j)),
        out_shape=jax.ShapeDtypeStruct((n, d), F32),
        compiler_params=_cparams(("arbitrary", "arbitrary")),
        name="outproj",
    )(o1, o2, w, w, x2, gate.reshape(bsz, 1, d))


PACK_SUB = 8
U32 = jnp.uint32
HIGH_HALF = np.uint32(0xFFFF0000)


def _pack_rows(vals, dst_ref, rows):
    half = vals.shape[1] // 2
    lo = lax.bitcast_convert_type(vals[:, :half].astype(BF16).astype(F32), U32)
    hi = lax.bitcast_convert_type(vals[:, half:].astype(BF16).astype(F32), U32)
    words = (lo >> 16) | (hi & HIGH_HALF)
    for s in range(PACK_SUB):
        dst_ref[pl.ds(s, rows, stride=PACK_SUB), :] = words[:, s * LANES:(s + 1) * LANES]


def _unpack_words(words):
    lo = lax.bitcast_convert_type(words << 16, F32)
    hi = lax.bitcast_convert_type(words & HIGH_HALF, F32)
    return lo, hi


def _unpack_rows(src_ref, rows):
    los, his = [], []
    for s in range(PACK_SUB):
        lo, hi = _unpack_words(src_ref[pl.ds(s, rows, stride=PACK_SUB), :])
        los.append(lo)
        his.append(hi)
    return jnp.concatenate(los + his, axis=1)


def _route_kernel(x_ref, g_ref, sh_ref, sc_ref, wr_ref, rb_ref,
                  h_ref, e8_ref, pos8_ref, gate8_ref, cnt_ref,
                  carry, tri_tok, tri_exp):
    i = pl.program_id(0)
    tm = ROUTE_TM
    per_group = N_EXPERTS // N_GROUPS

    @pl.when(i == 0)
    def _():
        carry[...] = jnp.zeros(carry.shape, F32)
        r = lax.broadcasted_iota(I32, (tm, tm), 0)
        c = lax.broadcasted_iota(I32, (tm, tm), 1)
        tri_tok[...] = jnp.where(r <= c, 1.0, 0.0).astype(BF16)
        re = lax.broadcasted_iota(I32, (N_EXPERTS, N_EXPERTS), 0)
        ce = lax.broadcasted_iota(I32, (N_EXPERTS, N_EXPERTS), 1)
        tri_exp[...] = jnp.where(ce < re, 1.0, 0.0).astype(BF16)

    h = _norm_modulate(x_ref[...], g_ref[...], sh_ref[0], sc_ref[0])
    _pack_rows(h, h_ref, tm)
    logits = lax.dot_general(wr_ref[...], h, (((1,), (1,)), ((), ())), precision=HIGHEST,
                             preferred_element_type=F32)
    scores = jax.nn.sigmoid(logits)
    sel = scores + rb_ref[...]

    sel3 = sel.reshape(N_GROUPS, per_group, tm)
    within = lax.broadcasted_iota(I32, sel3.shape, 1)
    max1 = jnp.max(sel3, axis=1, keepdims=True)
    first = jnp.min(jnp.where(sel3 == max1, within, per_group), axis=1, keepdims=True)
    max2 = jnp.max(jnp.where(within == first, -jnp.inf, sel3), axis=1, keepdims=True)
    gscore = max1 + max2

    gidx = lax.broadcasted_iota(I32, gscore.shape, 0)
    grank = jnp.zeros(gscore.shape, I32)
    for g in range(N_GROUPS):
        other = gscore[g:g + 1]
        beats = (other > gscore) | ((other == gscore) & (gidx > g))
        grank = grank + jnp.where(beats, 1, 0)
    keep = grank < TOPK_GROUPS
    masked = jnp.where(keep, sel3, NEG_INF).reshape(N_EXPERTS, tm)

    eidx = lax.broadcasted_iota(I32, (N_EXPERTS, tm), 0)
    erank = jnp.zeros((N_EXPERTS, tm), I32)
    for e in range(N_EXPERTS):
        other = masked[e:e + 1, :]
        beats = (other > masked) | ((other == masked) & (eidx > e))
        erank = erank + jnp.where(beats, 1, 0)
    chosen = erank < TOP_K

    gates = jnp.where(chosen, scores, 0.0)
    gates = gates / jnp.sum(gates, axis=0, keepdims=True) * ROUTED_SCALE

    chosen_b = jnp.where(chosen, 1.0, 0.0).astype(BF16)
    slot = jnp.dot(tri_exp[...], chosen_b, preferred_element_type=F32)
    csum = jnp.dot(chosen_b, tri_tok[...], preferred_element_type=F32)
    pos = carry[:, 0:1] + csum - 1.0
    eidx_f = eidx.astype(F32)
    e_rows, p_rows, g_rows = [], [], []
    for kk in range(TOP_K):
        pick = chosen & (slot == float(kk))
        e_rows.append(jnp.sum(jnp.where(pick, eidx_f, 0.0), axis=0, keepdims=True))
        p_rows.append(jnp.sum(jnp.where(pick, pos, 0.0), axis=0, keepdims=True))
        g_rows.append(jnp.sum(jnp.where(pick, gates, 0.0), axis=0, keepdims=True))
    e8_ref[...] = jnp.concatenate(e_rows, axis=0).astype(I32)
    pos8_ref[...] = jnp.concatenate(p_rows, axis=0).astype(I32)
    gate8_ref[...] = jnp.concatenate(g_rows, axis=0)
    total = carry[...] + csum[:, tm - 1:tm]
    carry[...] = total
    cnt_ref[...] = total


def _route(x2, seq, gain, shift, scale, w_router, router_bias):
    n, d = x2.shape
    tm = ROUTE_TM
    bsz = shift.shape[0]
    return pl.pallas_call(
        _route_kernel,
        grid=(n // tm,),
        in_specs=[
            pl.BlockSpec((tm, d), lambda i: (i, 0)),
            pl.BlockSpec((1, d), lambda i: (0, 0)),
            pl.BlockSpec((1, 1, d), lambda i: (i * tm // seq, 0, 0)),
            pl.BlockSpec((1, 1, d), lambda i: (i * tm // seq, 0, 0)),
            pl.BlockSpec((N_EXPERTS, d), lambda i: (0, 0)),
            pl.BlockSpec((N_EXPERTS, 1), lambda i: (0, 0)),
        ],
        out_specs=[
            pl.BlockSpec((tm * PACK_SUB, LANES), lambda i: (i, 0)),
            pl.BlockSpec((TOP_K, tm), lambda i: (0, i)),
            pl.BlockSpec((TOP_K, tm), lambda i: (0, i)),
            pl.BlockSpec((TOP_K, tm), lambda i: (0, i)),
            pl.BlockSpec((N_EXPERTS, LANES), lambda i: (0, 0)),
        ],
        out_shape=[
            jax.ShapeDtypeStruct((n * PACK_SUB, LANES), U32),
            jax.ShapeDtypeStruct((TOP_K, n), I32),
            jax.ShapeDtypeStruct((TOP_K, n), I32),
            jax.ShapeDtypeStruct((TOP_K, n), F32),
            jax.ShapeDtypeStruct((N_EXPERTS, LANES), F32),
        ],
        scratch_shapes=[
            pltpu.VMEM((N_EXPERTS, LANES), F32),
            pltpu.VMEM((tm, tm), BF16),
            pltpu.VMEM((N_EXPERTS, N_EXPERTS), BF16),
        ],
        compiler_params=_cparams(("arbitrary",)),
        name="route",
    )(x2, gain.reshape(1, d), shift.reshape(bsz, 1, d), scale.reshape(bsz, 1, d),
      w_router.T, router_bias.reshape(N_EXPERTS, 1))


def _tables_kernel(cnt_ref, e8_ref, pos8_ref, dest_ref, bexp_ref, ends_ref, *, nb_pad):
    counts = cnt_ref[...]
    blocks = jnp.ceil(counts / float(MOE_BM))
    re = lax.broadcasted_iota(I32, (N_EXPERTS, N_EXPERTS), 0)
    ce = lax.broadcasted_iota(I32, (N_EXPERTS, N_EXPERTS), 1)
    tri = jnp.where(ce < re, 1.0, 0.0).astype(BF16)
    start_blk = jnp.dot(tri, blocks.astype(BF16), preferred_element_type=F32)
    end_blk = start_blk + blocks
    e8 = e8_ref[...]
    dest = pos8_ref[...].astype(F32)
    for e in range(N_EXPERTS):
        dest = dest + jnp.where(e8 == e, start_blk[e:e + 1, 0:1] * float(MOE_BM), 0.0)
    dest_ref[...] = dest.astype(I32)
    bidx = lax.broadcasted_iota(I32, (N_EXPERTS, nb_pad), 1).astype(F32)
    owner = jnp.sum(jnp.where(end_blk[:, 0:1] <= bidx, 1.0, 0.0), axis=0, keepdims=True)
    bexp_ref[...] = jnp.minimum(owner, float(N_EXPERTS - 1)).astype(I32)
    ends_ref[...] = end_blk.astype(I32)


def _tables(counts, e8, pos8, nb):
    n = e8.shape[1]
    tm = 2048
    nb_pad = -(-nb // LANES) * LANES
    kern = functools.partial(_tables_kernel, nb_pad=nb_pad)
    return pl.pallas_call(
        kern,
        grid=(n // tm,),
        in_specs=[
            pl.BlockSpec((N_EXPERTS, LANES), lambda i: (0, 0)),
            pl.BlockSpec((TOP_K, tm), lambda i: (0, i)),
            pl.BlockSpec((TOP_K, tm), lambda i: (0, i)),
        ],
        out_specs=[
            pl.BlockSpec((TOP_K, tm), lambda i: (0, i)),
            pl.BlockSpec((1, nb_pad), lambda i: (0, 0)),
            pl.BlockSpec((N_EXPERTS, LANES), lambda i: (0, 0)),
        ],
        out_shape=[
            jax.ShapeDtypeStruct((TOP_K, n), I32),
            jax.ShapeDtypeStruct((1, nb_pad), I32),
            jax.ShapeDtypeStruct((N_EXPERTS, LANES), I32),
        ],
        compiler_params=_cparams(("arbitrary",)),
        name="tables",
    )(counts, e8, pos8)


def _dispatch_kernel(ends_ref, tab_hbm, h_ref, xs_hbm, tab_smem, zero_buf,
                     sem_tab, sem_rows, sem_zero, *, n_tiles, n_blocks):
    i = pl.program_id(0)
    slot = i % 2
    tm = DISP_TM
    blk_rows = MOE_BM * PACK_SUB

    def tab_copy(t, sl):
        return pltpu.make_async_copy(tab_hbm.at[t], tab_smem.at[sl], sem_tab.at[sl])

    def zero_copy(e):
        end = ends_ref[e]
        start = jnp.where(e > 0, ends_ref[jnp.maximum(e - 1, 0)], 0)
        row0 = pl.multiple_of(jnp.maximum(end - 1, 0) * blk_rows, PACK_SUB)
        return end > start, pltpu.make_async_copy(
            zero_buf, xs_hbm.at[pl.ds(row0, blk_rows), :], sem_zero)

    @pl.when(i == 0)
    def _():
        tab_copy(0, 0).start()
        zero_buf[...] = jnp.zeros(zero_buf.shape, U32)

        def start_body(e, carry):
            has_rows, cp = zero_copy(e)

            @pl.when(has_rows)
            def _():
                cp.start()
            return carry

        def wait_body(e, carry):
            has_rows, cp = zero_copy(e)

            @pl.when(has_rows)
            def _():
                cp.wait()
            return carry

        def tail_copy(blk):
            row0 = pl.multiple_of(blk * blk_rows, PACK_SUB)
            return pltpu.make_async_copy(zero_buf, xs_hbm.at[pl.ds(row0, blk_rows), :],
                                         sem_zero)

        def tail_start(blk, carry):
            tail_copy(blk).start()
            return carry

        def tail_wait(blk, carry):
            tail_copy(blk).wait()
            return carry

        n_used = ends_ref[N_EXPERTS - 1]
        lax.fori_loop(0, N_EXPERTS, start_body, 0)
        lax.fori_loop(n_used, n_blocks, tail_start, 0)
        lax.fori_loop(0, N_EXPERTS, wait_body, 0)
        lax.fori_loop(n_used, n_blocks, tail_wait, 0)

    tab_copy(i, slot).wait()

    @pl.when(i + 1 < n_tiles)
    def _():
        tab_copy(i + 1, 1 - slot).start()

    def body(r, carry):
        src = pl.multiple_of(r * PACK_SUB, PACK_SUB)
        for kk in range(TOP_K):
            dst = pl.multiple_of(tab_smem[slot, kk * tm + r] * PACK_SUB, PACK_SUB)
            pltpu.make_async_copy(h_ref.at[pl.ds(src, PACK_SUB), :],
                                  xs_hbm.at[pl.ds(dst, PACK_SUB), :], sem_rows).start()
        return carry

    lax.fori_loop(0, tm, body, 0, unroll=2)
    for kk in range(TOP_K):
        pltpu.make_async_copy(h_ref, xs_hbm.at[pl.ds(0, tm * PACK_SUB), :], sem_rows).wait()


def _dispatch(ends, tab, h_packed, nb):
    n_tiles = tab.shape[0]
    kern = functools.partial(_dispatch_kernel, n_tiles=n_tiles, n_blocks=nb)
    grid_spec = pltpu.PrefetchScalarGridSpec(
        num_scalar_prefetch=1,
        grid=(n_tiles,),
        in_specs=[pl.BlockSpec(memory_space=pl.ANY),
                  pl.BlockSpec((DISP_TM * PACK_SUB, LANES), lambda i, ends: (i, 0))],
        out_specs=pl.BlockSpec(memory_space=pl.ANY),
        scratch_shapes=[
            pltpu.SMEM((2, TOP_K * DISP_TM), I32),
            pltpu.VMEM((MOE_BM * PACK_SUB, LANES), U32),
            pltpu.SemaphoreType.DMA((2,)),
            pltpu.SemaphoreType.DMA,
            pltpu.SemaphoreType.DMA,
        ],
    )
    return pl.pallas_call(
        kern,
        grid_spec=grid_spec,
        out_shape=jax.ShapeDtypeStruct((nb * MOE_BM * PACK_SUB, LANES), U32),
        compiler_params=_cparams(("arbitrary",)),
        name="dispatch",
    )(ends, tab, h_packed)


def _experts_kernel(bexp_ref, nused_ref, xs_ref, wu_ref, wd_ref, y_ref, wu_bf, wd_bf):
    b = pl.program_id(0)
    nu = nused_ref[0]
    bm = MOE_BM

    prev_expert = bexp_ref[jnp.maximum(b - 1, 0)]

    @pl.when((b == 0) | (bexp_ref[b] != prev_expert))
    def _():
        wu_bf[...] = wu_ref[...].astype(BF16)
        wd_bf[...] = wd_ref[...].astype(BF16)

    @pl.when(b < nu)
    def _():
        x = _unpack_rows(xs_ref, bm).astype(BF16)
        up = jnp.dot(x, wu_bf[...], preferred_element_type=F32)
        act = _silu(up[:, :EXPERT_DIM]) * up[:, EXPERT_DIM:]
        y = jnp.dot(act.astype(BF16), wd_bf[...], preferred_element_type=F32)
        _pack_rows(y, y_ref, bm)

    @pl.when(b >= nu)
    def _():
        y_ref[...] = jnp.zeros(y_ref.shape, U32)


def _experts(bexp, nused, xs, w_up_all, w_down_all, layer):
    bm = MOE_BM
    nb = xs.shape[0] // (bm * PACK_SUB)
    d = w_up_all.shape[2]
    grid_spec = pltpu.PrefetchScalarGridSpec(
        num_scalar_prefetch=2,
        grid=(nb,),
        in_specs=[
            pl.BlockSpec((bm * PACK_SUB, LANES),
                         lambda b, be, nu: (jnp.minimum(b, nu[0] - 1), 0)),
            pl.BlockSpec((None, None, d, 2 * EXPERT_DIM),
                         lambda b, be, nu: (layer, be[b], 0, 0)),
            pl.BlockSpec((None, None, EXPERT_DIM, d),
                         lambda b, be, nu: (layer, be[b], 0, 0)),
        ],
        out_specs=pl.BlockSpec((bm * PACK_SUB, LANES), lambda b, be, nu: (b, 0)),
        scratch_shapes=[
            pltpu.VMEM((d, 2 * EXPERT_DIM), BF16),
            pltpu.VMEM((EXPERT_DIM, d), BF16),
        ],
    )
    return pl.pallas_call(
        _experts_kernel,
        grid_spec=grid_spec,
        out_shape=jax.ShapeDtypeStruct((nb * bm * PACK_SUB, LANES), U32),
        compiler_params=_cparams(("arbitrary",)),
        name="experts",
    )(bexp, nused, xs, w_up_all, w_down_all)


def _combine_kernel(tab_hbm, y_hbm, gate_ref, h_ref, x_ref, gf_ref, wsu_ref, wsd_ref, out_ref,
                    ybuf, tab_smem, shared_scr, sem_tab, sem_rows, *, n_tiles):
    i = pl.program_id(0)
    slot = i % 2
    tm = COMB_TM

    def tab_copy(t, sl):
        return pltpu.make_async_copy(tab_hbm.at[t], tab_smem.at[sl], sem_tab.at[sl])

    def issue_rows(sl):
        for kk in range(TOP_K):
            def body(r, carry, kk=kk):
                src = pl.multiple_of(tab_smem[sl, kk * tm + r] * PACK_SUB, PACK_SUB)
                dst = pl.multiple_of(r * PACK_SUB, PACK_SUB)
                pltpu.make_async_copy(y_hbm.at[pl.ds(src, PACK_SUB), :],
                                      ybuf.at[sl, kk, pl.ds(dst, PACK_SUB), :],
                                      sem_rows.at[sl]).start()
                return carry
            lax.fori_loop(0, tm, body, 0, unroll=8)

    @pl.when(i == 0)
    def _():
        tab_copy(0, 0).start()
        tab_copy(0, 0).wait()
        issue_rows(0)
        if n_tiles > 1:
            tab_copy(1, 1).start()

    @pl.when(i + 1 < n_tiles)
    def _():
        tab_copy(i + 1, 1 - slot).wait()
        issue_rows(1 - slot)

    @pl.when(i + 2 < n_tiles)
    def _():
        tab_copy(i + 2, slot).start()

    hb = _unpack_rows(h_ref, tm).astype(BF16)
    up = jnp.dot(hb, wsu_ref[...], preferred_element_type=F32)
    half = wsd_ref.shape[0]
    act = _silu(up[:, :half]) * up[:, half:]
    shared_scr[...] = jnp.dot(act.astype(BF16), wsd_ref[...], preferred_element_type=F32)

    for kk in range(TOP_K):
        pltpu.make_async_copy(y_hbm.at[pl.ds(0, tm * PACK_SUB), :], ybuf.at[slot, kk],
                              sem_rows.at[slot]).wait()
    gate = gate_ref[...]
    gates = [jnp.broadcast_to(gate[:, kk:kk + 1], (tm, LANES)) for kk in range(TOP_K)]
    d_half = PACK_SUB * LANES
    for s in range(PACK_SUB):
        acc_lo = jnp.zeros((tm, LANES), F32)
        acc_hi = jnp.zeros((tm, LANES), F32)
        for kk in range(TOP_K):
            lo, hi = _unpack_words(ybuf[slot, kk, pl.ds(s, tm, stride=PACK_SUB), :])
            acc_lo = acc_lo + gates[kk] * lo
            acc_hi = acc_hi + gates[kk] * hi
        for acc, c0 in ((acc_lo, s * LANES), (acc_hi, d_half + s * LANES)):
            cs = slice(c0, c0 + LANES)
            out_ref[:, cs] = x_ref[:, cs] + gf_ref[0, :, cs] * (acc + shared_scr[:, cs])


def _combine(tab, y, gate_t, h, x2, g_f, w_su, w_sd, seq):
    n, d = x2.shape
    tm = COMB_TM
    n_tiles = n // tm
    bsz = g_f.shape[0]
    kern = functools.partial(_combine_kernel, n_tiles=n_tiles)
    return pl.pallas_call(
        kern,
        grid=(n_tiles,),
        in_specs=[
            pl.BlockSpec(memory_space=pl.ANY),
            pl.BlockSpec(memory_space=pl.ANY),
            pl.BlockSpec((tm, TOP_K), lambda i: (i, 0)),
            pl.BlockSpec((tm * PACK_SUB, LANES), lambda i: (i, 0)),
            pl.BlockSpec((tm, d), lambda i: (i, 0)),
            pl.BlockSpec((1, 1, d), lambda i: (i * tm // seq, 0, 0)),
            pl.BlockSpec(w_su.shape, lambda i: (0, 0)),
            pl.BlockSpec(w_sd.shape, lambda i: (0, 0)),
        ],
        out_specs=pl.BlockSpec((tm, d), lambda i: (i, 0)),
        out_shape=jax.ShapeDtypeStruct((n, d), F32),
        scratch_shapes=[
            pltpu.VMEM((2, TOP_K, tm * PACK_SUB, LANES), U32),
            pltpu.SMEM((2, TOP_K * tm), I32),
            pltpu.VMEM((tm, d), F32),
            pltpu.SemaphoreType.DMA((2,)),
            pltpu.SemaphoreType.DMA((2,)),
        ],
        compiler_params=_cparams(("arbitrary",)),
        name="combine",
    )(tab, y, gate_t, h, x2, g_f.reshape(bsz, 1, d), w_su, w_sd)


def _tile_table(dest8, tm):
    n = dest8.shape[1]
    return dest8.reshape(TOP_K, n // tm, tm).transpose(1, 0, 2).reshape(n // tm, TOP_K * tm)


def _moe(x2, seq, gain, shift, scale, g_f, w_router, router_bias, w_up_all, w_down_all, layer,
         w_su, w_sd):
    n, d = x2.shape
    assert d == 2 * PACK_SUB * LANES
    h, e8, pos8, gate8, counts = _route(x2, seq, gain, shift, scale, w_router, router_bias)
    nb = -(-(n * TOP_K + N_EXPERTS * (MOE_BM - 1)) // MOE_BM)
    dest8, bexp, ends = _tables(counts, e8, pos8, nb)
    ends = ends[:, 0]
    xs = _dispatch(ends, _tile_table(dest8, DISP_TM), h, nb)
    y = _experts(bexp[0, :nb], ends[N_EXPERTS - 1:], xs, w_up_all, w_down_all, layer)
    return _combine(_tile_table(dest8, COMB_TM), y, gate8.T, h, x2, g_f, w_su.astype(BF16),
                    w_sd.astype(BF16), seq)


def _alibi_slopes(nh):
    return jnp.asarray(2.0 ** (-8.0 * np.arange(1, nh + 1) / nh), dtype=F32)


def _even_weights(w_in, qk_norm):
    a_w = A_HEADS * HEAD_DIM
    b_w = B_HEADS * HEAD_DIM
    kv_w = B_KV_HEADS * HEAD_DIM
    cuts = np.cumsum([a_w, a_w, a_w, b_w, kv_w])
    qa, ka, va, qb, kb, vb = jnp.split(w_in, cuts, axis=1)

    def dup(a):
        return jnp.concatenate([a[:, hh * HEAD_DIM:(hh + 1) * HEAD_DIM]
                                for hh in range(B_KV_HEADS) for _ in range(2)], axis=1)

    w = jnp.concatenate([qa, ka, qb, dup(kb), va, dup(vb)], axis=1).astype(BF16)
    q_scale = HEAD_DIM ** -0.5 * LOG2E
    gains = jnp.concatenate([
        jnp.tile(qk_norm[0] * q_scale, A_HEADS), jnp.tile(qk_norm[1], A_HEADS),
        jnp.tile(qk_norm[2] * q_scale, B_HEADS), jnp.tile(qk_norm[3], 2 * B_KV_HEADS),
        jnp.ones((a_w + 2 * kv_w,), F32)])
    n_norm = 2 * a_w + b_w + 2 * kv_w
    return w, gains, n_norm


def kernel(x, c, norm_mix, norm_ffn, w_ada, b_ada, ab_w_in, ab_qk_norm, ab_sinks, ab_w_out,
           c_w_in, c_qk_norm, c_lambda, c_subln, c_w_out, w_router, router_bias,
           experts_up, experts_down, shared_up, shared_down):
    bsz, seq, d = x.shape
    depth = w_ada.shape[0]
    n = bsz * seq
    ada = _ada(c, w_ada, b_ada)
    x2 = x.reshape(n, d)
    for i in range(depth):
        sh_m, sc_m, g_m, sh_f, sc_f, g_f = jnp.split(ada[i], 6, axis=-1)
        j = i // 2
        if i % 2 == 0:
            w, gains, n_norm = _even_weights(ab_w_in[j], ab_qk_norm[j])
            qkv = _proj(x2, seq, norm_mix[i], sh_m, sc_m, w, gains, n_norm)
            pa = A_HEADS // 2
            pb = B_HEADS // 2
            q_a, k_a, q_b, k_b = 0, pa, 2 * pa, 2 * pa + pb
            v_a = k_b + B_KV_HEADS
            v_b = v_a + pa
            o_a = _attn_a(qkv, bsz, seq, q_a, k_a, v_a, _alibi_slopes(A_HEADS))
            o_b = _attn_b(qkv, bsz, seq, q_b, k_b, v_b, _alibi_slopes(B_HEADS),
                          ab_sinks[j].astype(F32))
            x2 = _outproj(o_a.reshape(n, -1), 0, o_b.reshape(n, -1), 0,
                          ab_w_out[j].astype(BF16), x2, g_m, seq)
        else:
            q_scale = HEAD_DIM ** -0.5 * LOG2E
            gains = jnp.concatenate([
                jnp.tile(c_qk_norm[j, 0] * q_scale, 2 * C_HEADS),
                jnp.tile(c_qk_norm[j, 1], 2 * C_HEADS),
                jnp.ones((2 * C_HEADS * HEAD_DIM,), F32)])
            qkv = _proj(x2, seq, norm_mix[i], sh_m, sc_m, c_w_in[j].astype(BF16), gains,
                        4 * C_HEADS * HEAD_DIM)
            lam_init = 0.8 - 0.6 * math.exp(-0.3 * i)
            o_c = _attn_c(qkv, bsz, seq, _alibi_slopes(C_HEADS), c_qk_norm[j],
                          c_lambda[j].astype(F32),
                          c_subln[j].astype(F32), lam_init).reshape(n, -1)
            x2 = _outproj(o_c, 0, o_c, 1, c_w_out[j].astype(BF16), x2, g_m, seq)
        x2 = _moe(x2, seq, norm_ffn[i], sh_f, sc_f, g_f, w_router[i], router_bias[i],
                  experts_up, experts_down, i, shared_up[i], shared_down[i])
    return x2.reshape(bsz, seq, d)
```

```python
import functools
import math

import jax
import jax.numpy as jnp
import numpy as np
from jax import lax
from jax.experimental import pallas as pl
from jax.experimental.pallas import tpu as pltpu

F32 = jnp.float32
BF16 = jnp.bfloat16
I32 = jnp.int32
HIGHEST = lax.Precision.HIGHEST

HEAD_DIM = 64
A_HEADS = 16
DILATIONS = (1, 4, 16)
A_MAX_DIST = 128
B_HEADS = 16
B_KV_HEADS = 2
B_MAX_DIST = 127
C_HEADS = 16
N_EXPERTS = 64
N_GROUPS = 8
TOPK_GROUPS = 4
TOP_K = 8
EXPERT_DIM = 512
ROUTED_SCALE = 2.5
RMS_EPS = 1e-6
NEG_INF = -1e30
LOG2E = math.log2(math.e)
SCORE_BOUND_MARGIN = 1.02
MAX_BOUNDED_EXPONENT = 100.0

LANES = 128
VMEM_LIMIT = 56 * 1024 * 1024

PROJ_TM = 1024
PROJ_TN = 512
ATT_TILE = 2048
BAND = 128
DIFF_TQ = 1024
DIFF_TK = 1024
ROUTE_TM = 512
MOE_BM = 512
COMB_TM = 256
DISP_TM = 512


def _cparams(sem):
    return pltpu.CompilerParams(dimension_semantics=sem, vmem_limit_bytes=VMEM_LIMIT)


def _silu(x):
    return x * jax.nn.sigmoid(x)


def _ada_kernel(c_ref, w_ref, b_ref, o_ref):
    cond = _silu(c_ref[...])
    o_ref[...] = jnp.dot(cond, w_ref[...], precision=HIGHEST,
                         preferred_element_type=F32) + b_ref[...]


def _ada(c, w_ada, b_ada):
    depth, d, d6 = w_ada.shape
    b = c.shape[0]
    bp = 8
    cp = jnp.zeros((bp, d), F32).at[:b].set(c)
    tn = 1024
    out = pl.pallas_call(
        _ada_kernel,
        grid=(depth, d6 // tn),
        in_specs=[
            pl.BlockSpec((bp, d), lambda i, j: (0, 0)),
            pl.BlockSpec((None, d, tn), lambda i, j: (i, 0, j)),
            pl.BlockSpec((None, 1, tn), lambda i, j: (i, 0, j)),
        ],
        out_specs=pl.BlockSpec((None, bp, tn), lambda i, j: (i, 0, j)),
        out_shape=jax.ShapeDtypeStruct((depth, bp, d6), F32),
        compiler_params=_cparams(("arbitrary", "arbitrary")),
        name="ada",
    )(cp, w_ada, b_ada.reshape(depth, 1, d6))
    return out[:, :b]


def _norm_modulate(x, g, sh, sc):
    ms = jnp.mean(x * x, axis=-1, keepdims=True)
    y = x * lax.rsqrt(ms + RMS_EPS) * g
    return y * (1.0 + sc) + sh


def _proj_kernel(x_ref, g_ref, sh_ref, sc_ref, w_ref, hg_ref, pm_ref, o_ref, h_scr, *,
                 n_norm_cols, tn):
    j = pl.program_id(1)

    @pl.when(j == 0)
    def _():
        h = _norm_modulate(x_ref[...], g_ref[...], sh_ref[0], sc_ref[0])
        h_scr[...] = h.astype(BF16)

    acc = jnp.dot(h_scr[...], w_ref[...], preferred_element_type=F32)
    for cb in range(tn // LANES):
        a = acc[:, cb * LANES:(cb + 1) * LANES]
        is_norm = j * tn + cb * LANES < n_norm_cols

        @pl.when(is_norm)
        def _():
            sq = a * a
            hi = sq.astype(BF16)
            lo = (sq - hi.astype(F32)).astype(BF16)
            ms = (jnp.dot(hi, pm_ref[...], preferred_element_type=F32)
                  + jnp.dot(lo, pm_ref[...], preferred_element_type=F32))
            y = a * lax.rsqrt(ms + RMS_EPS) * hg_ref[:, cb * LANES:(cb + 1) * LANES]
            o_ref[cb] = y.astype(BF16)

        @pl.when(jnp.logical_not(is_norm))
        def _():
            o_ref[cb] = a.astype(BF16)


def _head_mean_matrix():
    r = np.arange(LANES)
    pm = (r[:, None] // HEAD_DIM == r[None, :] // HEAD_DIM).astype(np.float32) / HEAD_DIM
    return jnp.asarray(pm, BF16)


def _proj(x2, seq, gain, shift, scale, w, head_gain, n_norm_cols):
    n, d = x2.shape
    cols = w.shape[1]
    tm, tn = PROJ_TM, PROJ_TN
    bsz = shift.shape[0]
    kern = functools.partial(_proj_kernel, n_norm_cols=n_norm_cols, tn=tn)
    return pl.pallas_call(
        kern,
        grid=(n // tm, cols // tn),
        in_specs=[
            pl.BlockSpec((tm, d), lambda i, j: (i, 0)),
            pl.BlockSpec((1, d), lambda i, j: (0, 0)),
            pl.BlockSpec((1, 1, d), lambda i, j: (i * tm // seq, 0, 0)),
            pl.BlockSpec((1, 1, d), lambda i, j: (i * tm // seq, 0, 0)),
            pl.BlockSpec((d, tn), lambda i, j: (0, j)),
            pl.BlockSpec((1, tn), lambda i, j: (0, j)),
            pl.BlockSpec((LANES, LANES), lambda i, j: (0, 0)),
        ],
        out_specs=pl.BlockSpec((tn // LANES, tm, LANES), lambda i, j: (j, i, 0)),
        out_shape=jax.ShapeDtypeStruct((cols // LANES, n, LANES), BF16),
        scratch_shapes=[pltpu.VMEM((tm, d), BF16)],
        compiler_params=_cparams(("arbitrary", "arbitrary")),
        name="proj",
    )(x2, gain.reshape(1, d), shift.reshape(bsz, 1, d), scale.reshape(bsz, 1, d), w,
      head_gain.reshape(1, cols), _head_mean_matrix())


def _band_block(qz, kk, v_aug, bias, sink=None):
    s = lax.dot_general(qz, kk, (((1,), (1,)), ((), ())), preferred_element_type=F32) + bias
    m = jnp.max(s, axis=-1, keepdims=True)
    if sink is not None:
        m = jnp.maximum(m, sink)
    p = jnp.exp2(s - m)
    res = jnp.dot(p.astype(BF16), v_aug, preferred_element_type=F32)
    return res, m


def _band_masks(max_dist):
    qi = lax.broadcasted_iota(I32, (BAND, 2 * BAND), 0)
    kj = lax.broadcasted_iota(I32, (BAND, 2 * BAND), 1)
    dist = BAND + qi - kj
    valid = (dist >= 0) & (dist <= max_dist)
    valid_first = valid & (kj >= BAND)
    return dist.astype(F32), valid, valid_first


def _attn_a_kernel(slopes_ref,
                   q1, q4, q16,
                   k1c, k1p, k4c, k4p, k16c, k16p,
                   v1c, v1p, v4c, v4p, v16c, v16p,
                   o_ref, kcat, vcat, st):
    p = pl.program_id(1)
    n = pl.program_id(2)
    lane = lax.broadcasted_iota(I32, (BAND, LANES), 1)
    distf, valid, valid_first = _band_masks(A_MAX_DIST)
    seq_start = n == 0
    branches = ((1, q1, k1c, k1p, v1c, v1p), (4, q4, k4c, k4p, v4c, v4p),
                (16, q16, k16c, k16p, v16c, v16p))
    for bi, (d, qv, kc, kp, vc, vp) in enumerate(branches):
        n_l = ATT_TILE // BAND // d
        rows = n_l * BAND
        biases = []
        for h in range(2):
            slope = slopes_ref[2 * p + h] * (float(d) * LOG2E)
            b_full = jnp.where(valid, -slope * distf, NEG_INF)
            b_first = jnp.where(valid_first, -slope * distf, NEG_INF)
            biases.append((b_full, jnp.where(seq_start, b_first, b_full)))
        for r in range(d):
            cs = slice(r * LANES, (r + 1) * LANES)
            kcat[0:BAND, :] = kp[:, cs]
            kcat[BAND:BAND + rows, :] = kc[:, cs]
            vprev = vp[:, cs]
            vcur = vc[:, cs]
            for h in range(2):
                vcat[h, 0:BAND, :] = jnp.where(lane // HEAD_DIM == h, vprev, 1.0).astype(BF16)
                lane_k = lax.broadcasted_iota(I32, (rows, LANES), 1)
                vcat[h, BAND:BAND + rows, :] = jnp.where(
                    lane_k // HEAD_DIM == h, vcur, 1.0).astype(BF16)
            for h in range(2):
                m_lane = LANES - 1 if h == 0 else 0
                for jb in range(n_l):
                    qb = qv[jb * BAND:(jb + 1) * BAND, cs]
                    qz = jnp.where(lane // HEAD_DIM == h, qb, 0.0).astype(BF16)
                    kk = kcat[jb * BAND:jb * BAND + 2 * BAND, :]
                    vv = vcat[h, jb * BAND:jb * BAND + 2 * BAND, :]
                    bias = biases[h][1] if jb == 0 else biases[h][0]
                    res, m = _band_block(qz, kk, vv, bias)
                    res = jnp.where(lane == m_lane, m, res)
                    if d == 1:
                        st[bi, h, jb * BAND:(jb + 1) * BAND, :] = res
                    else:
                        st[bi, h, pl.ds(jb * BAND * d + r, BAND, stride=d), :] = res
    for ib in range(ATT_TILE // BAND):
        rs = slice(ib * BAND, (ib + 1) * BAND)
        for h in range(2):
            m_lane = LANES - 1 if h == 0 else 0
            l_lane = HEAD_DIM if h == 0 else 1
            hs = slice(h * HEAD_DIM, (h + 1) * HEAD_DIM)
            parts = [st[bi, h, rs, :] for bi in range(3)]
            ms = [x[:, m_lane:m_lane + 1] for x in parts]
            m_all = jnp.maximum(jnp.maximum(ms[0], ms[1]), ms[2])
            num = sum(jnp.exp2(mm - m_all) * x for mm, x in zip(ms, parts))
            out = num / num[:, l_lane:l_lane + 1]
            o_ref[rs, hs] = out[:, hs].astype(BF16)


def _attn_a(qkv, bsz, seq, q_cb, k_cb, v_cb, slopes):
    cb = qkv.shape[0]
    views = {d: qkv.reshape(cb, bsz, seq // d, d * LANES) for d in DILATIONS}
    n_pairs = A_HEADS // 2
    tile = ATT_TILE
    in_specs = [pl.BlockSpec(memory_space=pltpu.SMEM)]
    args = [slopes]
    for d in DILATIONS:
        in_specs.append(pl.BlockSpec((None, None, tile // d, d * LANES),
                                     lambda b, p, n: (q_cb + p, b, n, 0)))
        args.append(views[d])
    for base in (k_cb, v_cb):
        for d in DILATIONS:
            per = tile // d // BAND
            in_specs.append(pl.BlockSpec((None, None, tile // d, d * LANES),
                                         lambda b, p, n, base=base: (base + p, b, n, 0)))
            in_specs.append(pl.BlockSpec(
                (None, None, BAND, d * LANES),
                lambda b, p, n, base=base, per=per: (base + p, b, jnp.maximum(n * per - 1, 0), 0)))
            args += [views[d], views[d]]
    return pl.pallas_call(
        _attn_a_kernel,
        grid=(bsz, n_pairs, seq // tile),
        in_specs=in_specs,
        out_specs=pl.BlockSpec((None, tile, LANES), lambda b, p, n: (b, n, p)),
        out_shape=jax.ShapeDtypeStruct((bsz, seq, A_HEADS * HEAD_DIM), BF16),
        scratch_shapes=[
            pltpu.VMEM((BAND + tile, LANES), BF16),
            pltpu.VMEM((2, BAND + tile, LANES), BF16),
            pltpu.VMEM((3, 2, tile, LANES), F32),
        ],
        compiler_params=_cparams(("arbitrary", "arbitrary", "arbitrary")),
        name="attn_dilated",
    )(*args)


def _attn_b_kernel(slopes_ref, sinks_ref, q_ref, kc, kp, vc, vp, o_ref, kcat, vcat):
    p = pl.program_id(1)
    n = pl.program_id(2)
    lane = lax.broadcasted_iota(I32, (BAND, LANES), 1)
    lane_k = lax.broadcasted_iota(I32, (ATT_TILE, LANES), 1)
    distf, valid, valid_first = _band_masks(B_MAX_DIST)
    seq_start = n == 0
    n_l = ATT_TILE // BAND
    kcat[0:BAND, :] = kp[...]
    kcat[BAND:, :] = kc[...]
    vprev = vp[...]
    vcur = vc[...]
    for h in range(2):
        vcat[h, 0:BAND, :] = jnp.where(lane // HEAD_DIM == h, vprev, 1.0).astype(BF16)
        vcat[h, BAND:, :] = jnp.where(lane_k // HEAD_DIM == h, vcur, 1.0).astype(BF16)
    for h in range(2):
        slope = slopes_ref[2 * p + h] * LOG2E
        sink = sinks_ref[2 * p + h] * LOG2E
        b_full = jnp.where(valid, -slope * distf, NEG_INF)
        b_first = jnp.where(seq_start, jnp.where(valid_first, -slope * distf, NEG_INF), b_full)
        l_lane = HEAD_DIM if h == 0 else 0
        for jb in range(n_l):
            qb = q_ref[jb * BAND:(jb + 1) * BAND, :]
            qz = jnp.where(lane // HEAD_DIM == h, qb, 0.0).astype(BF16)
            kk = kcat[jb * BAND:jb * BAND + 2 * BAND, :]
            vv = vcat[h, jb * BAND:jb * BAND + 2 * BAND, :]
            res, m = _band_block(qz, kk, vv, b_first if jb == 0 else b_full, sink=sink)
            denom = res[:, l_lane:l_lane + 1] + jnp.exp2(sink - m)
            hs = slice(h * HEAD_DIM, (h + 1) * HEAD_DIM)
            o_ref[jb * BAND:(jb + 1) * BAND, hs] = (res / denom)[:, hs].astype(BF16)


def _attn_b(qkv, bsz, seq, q_cb, k_cb, v_cb, slopes, sinks):
    cb = qkv.shape[0]
    view = qkv.reshape(cb, bsz, seq, LANES)
    n_pairs = B_HEADS // 2
    pairs_per_kv = n_pairs // B_KV_HEADS
    tile = ATT_TILE
    per = tile // BAND

    def cur(base):
        return pl.BlockSpec((None, None, tile, LANES),
                            lambda b, p, n: (base + p // pairs_per_kv, b, n, 0))

    def prev(base):
        return pl.BlockSpec(
            (None, None, BAND, LANES),
            lambda b, p, n: (base + p // pairs_per_kv, b, jnp.maximum(n * per - 1, 0), 0))

    return pl.pallas_call(
        _attn_b_kernel,
        grid=(bsz, n_pairs, seq // tile),
        in_specs=[
            pl.BlockSpec(memory_space=pltpu.SMEM),
            pl.BlockSpec(memory_space=pltpu.SMEM),
            pl.BlockSpec((None, None, tile, LANES), lambda b, p, n: (q_cb + p, b, n, 0)),
            cur(k_cb), prev(k_cb), cur(v_cb), prev(v_cb),
        ],
        out_specs=pl.BlockSpec((None, tile, LANES), lambda b, p, n: (b, n, p)),
        out_shape=jax.ShapeDtypeStruct((bsz, seq, B_HEADS * HEAD_DIM), BF16),
        scratch_shapes=[
            pltpu.VMEM((BAND + tile, LANES), BF16),
            pltpu.VMEM((2, BAND + tile, LANES), BF16),
        ],
        compiler_params=_cparams(("arbitrary", "arbitrary", "arbitrary")),
        name="attn_swa",
    )(slopes, sinks, view, view, view, view, view)


def _attn_c_kernel(qi_tab, ki_tab, bounded_ref, consts_ref, lam_ref, subln_ref,
                   q_ref, k_ref, v_ref, o_ref, m_scr, l_scr, acc_scr, bias_scr, *, lam_init):
    h = pl.program_id(1)
    step = pl.program_id(2)
    qi = qi_tab[step]
    ki = ki_tab[step]
    tq, tk = DIFF_TQ, DIFF_TK
    slope2 = consts_ref[h] * LOG2E
    bound2 = consts_ref[C_HEADS]
    bounded = bounded_ref[0] == 1
    tile_off = slope2 * ((ki - qi) * tk).astype(F32)

    @pl.when(step == 0)
    def _():
        rel = (lax.broadcasted_iota(I32, (tq, tk), 1)
               - lax.broadcasted_iota(I32, (tq, tk), 0)).astype(F32)
        bias = slope2 * rel
        bias_scr[0] = bias
        bias_scr[1] = jnp.where(rel <= 0.0, bias, NEG_INF)

    @pl.when(ki == 0)
    def _():
        m_scr[...] = jnp.full(m_scr.shape, NEG_INF, F32)
        l_scr[...] = jnp.zeros(l_scr.shape, F32)
        acc_scr[...] = jnp.zeros(acc_scr.shape, F32)

    def scores(a, q, k, lane):
        qz = jnp.where(lane // HEAD_DIM == a, q, 0.0).astype(BF16)
        return lax.dot_general(qz, k, (((1,), (1,)), ((), ())), preferred_element_type=F32)

    def update_online(bias_idx):
        q = q_ref[...]
        k = k_ref[...]
        v = v_ref[...]
        lane = lax.broadcasted_iota(I32, (tq, LANES), 1)
        for a in range(2):
            s = scores(a, q, k, lane) + bias_scr[bias_idx]
            m_prev = m_scr[a]
            m_new = jnp.maximum(m_prev, jnp.max(s, axis=-1, keepdims=True) + tile_off)
            alpha = jnp.exp2(m_prev - m_new)
            pr = jnp.exp2(s - (m_new - tile_off))
            l_scr[a] = alpha * l_scr[a] + jnp.sum(pr, axis=-1, keepdims=True)
            acc_scr[a, :, 0:LANES] = alpha * acc_scr[a, :, 0:LANES] + jnp.dot(
                pr.astype(BF16), v, preferred_element_type=F32)
            m_scr[a] = m_new

    def update_bounded(bias_idx):
        q = q_ref[...]
        k = k_ref[...]
        v = v_ref[...]
        lane = lax.broadcasted_iota(I32, (tq, LANES), 1)
        bias = bias_scr[bias_idx] + (tile_off - bound2)
        v_aug = jnp.concatenate([v, jnp.ones_like(v)], axis=1)
        for a in range(2):
            pr = jnp.exp2(scores(a, q, k, lane) + bias).astype(BF16)
            acc_scr[a] = acc_scr[a] + jnp.dot(pr, v_aug, preferred_element_type=F32)

    def finish(o1, o2):
        lp = lam_ref[...]
        lam = (jnp.exp(jnp.sum(lp[0:1] * lp[1:2], axis=-1, keepdims=True))
               - jnp.exp(jnp.sum(lp[2:3] * lp[3:4], axis=-1, keepdims=True)) + lam_init)
        o = o1 - lam * o2
        ms = jnp.mean(o * o, axis=-1, keepdims=True)
        o = o * lax.rsqrt(ms + RMS_EPS) * subln_ref[...] * (1.0 - lam_init)
        o_ref[...] = o.astype(BF16)

    @pl.when(bounded & (ki < qi))
    def _():
        update_bounded(0)

    @pl.when(bounded & (ki == qi))
    def _():
        update_bounded(1)
        finish(acc_scr[0, :, 0:LANES] / acc_scr[0, :, LANES:],
               acc_scr[1, :, 0:LANES] / acc_scr[1, :, LANES:])

    @pl.when(jnp.logical_not(bounded) & (ki < qi))
    def _():
        update_online(0)

    @pl.when(jnp.logical_not(bounded) & (ki == qi))
    def _():
        update_online(1)
        finish(acc_scr[0, :, 0:LANES] / l_scr[0], acc_scr[1, :, 0:LANES] / l_scr[1])


def _attn_c(qkv, bsz, seq, slopes, qk_gains, lam_params, subln, lam_init):
    cb = qkv.shape[0]
    view = qkv.reshape(cb, bsz, seq, LANES)
    tq, tk = DIFF_TQ, DIFF_TK
    assert tq == tk
    nq = seq // tq
    pairs = [(i, j) for i in range(nq) for j in range(i + 1)]
    qi_tab = jnp.asarray([p[0] for p in pairs], I32)
    ki_tab = jnp.asarray([p[1] for p in pairs], I32)
    g_q = jnp.max(jnp.abs(qk_gains[0].astype(F32)))
    g_k = jnp.max(jnp.abs(qk_gains[1].astype(F32)))
    bound2 = SCORE_BOUND_MARGIN * HEAD_DIM ** 0.5 * g_q * g_k * LOG2E
    bounded = (2.0 * bound2 < MAX_BOUNDED_EXPONENT).astype(I32).reshape(1)
    consts = jnp.concatenate([slopes, bound2.reshape(1)])
    kern = functools.partial(_attn_c_kernel, lam_init=lam_init)

    def im(fn):
        return lambda b, h, s, qt, kt, bd: fn(b, h, s, qt, kt)

    grid_spec = pltpu.PrefetchScalarGridSpec(
        num_scalar_prefetch=3,
        grid=(bsz, C_HEADS, len(pairs)),
        in_specs=[
            pl.BlockSpec(memory_space=pltpu.SMEM),
            pl.BlockSpec((4, HEAD_DIM), im(lambda b, h, s, qt, kt: (0, 0))),
            pl.BlockSpec((1, 2 * HEAD_DIM), im(lambda b, h, s, qt, kt: (0, 0))),
            pl.BlockSpec((None, None, tq, LANES), im(lambda b, h, s, qt, kt: (h, b, qt[s], 0))),
            pl.BlockSpec((None, None, tk, LANES),
                         im(lambda b, h, s, qt, kt: (C_HEADS + h, b, kt[s], 0))),
            pl.BlockSpec((None, None, tk, LANES),
                         im(lambda b, h, s, qt, kt: (2 * C_HEADS + h, b, kt[s], 0))),
        ],
        out_specs=pl.BlockSpec((None, tq, LANES), im(lambda b, h, s, qt, kt: (b, qt[s], h))),
        scratch_shapes=[
            pltpu.VMEM((2, tq, 1), F32),
            pltpu.VMEM((2, tq, 1), F32),
            pltpu.VMEM((2, tq, 2 * LANES), F32),
            pltpu.VMEM((2, tq, tk), F32),
        ],
    )
    return pl.pallas_call(
        kern,
        grid_spec=grid_spec,
        out_shape=jax.ShapeDtypeStruct((bsz, seq, C_HEADS * 2 * HEAD_DIM), BF16),
        compiler_params=_cparams(("arbitrary", "arbitrary", "arbitrary")),
        name="attn_diff",
    )(qi_tab, ki_tab, bounded, consts, lam_params, subln.reshape(1, 2 * HEAD_DIM),
      view, view, view)


def _outproj_kernel(o1_ref, o2_ref, w1_ref, w2_ref, x_ref, g_ref, out_ref):
    acc = (jnp.dot(o1_ref[...], w1_ref[...], preferred_element_type=F32)
           + jnp.dot(o2_ref[...], w2_ref[...], preferred_element_type=F32))
    out_ref[...] = x_ref[...] + g_ref[0] * acc


def _outproj(o1, cb1, o2, cb2, w, x2, gate, seq):
    n, d = x2.shape
    half = w.shape[0] // 2
    tm, tn = PROJ_TM, PROJ_TN
    bsz = gate.shape[0]
    return pl.pallas_call(
        _outproj_kernel,
        grid=(n // tm, d // tn),
        in_specs=[
            pl.BlockSpec((tm, half), lambda i, j: (i, cb1)),
            pl.BlockSpec((tm, half), lambda i, j: (i, cb2)),
            pl.BlockSpec((half, tn), lambda i, j: (0, j)),
            pl.BlockSpec((half, tn), lambda i, j: (1, j)),
            pl.BlockSpec((tm, tn), lambda i, j: (i, j)),
            pl.BlockSpec((1, 1, tn), lambda i, j: (i * tm // seq, 0, j)),
        ],
        out_specs=pl.BlockSpec((tm, tn), lambda i, j: (i, j)),
        out_shape=jax.ShapeDtypeStruct((n, d), F32),
        compiler_params=_cparams(("arbitrary", "arbitrary")),
        name="outproj",
    )(o1, o2, w, w, x2, gate.reshape(bsz, 1, d))


PACK_SUB = 8
U32 = jnp.uint32
HIGH_HALF = np.uint32(0xFFFF0000)


def _pack_rows(vals, dst_ref, rows):
    half = vals.shape[1] // 2
    lo = lax.bitcast_convert_type(vals[:, :half].astype(BF16).astype(F32), U32)
    hi = lax.bitcast_convert_type(vals[:, half:].astype(BF16).astype(F32), U32)
    words = (lo >> 16) | (hi & HIGH_HALF)
    for s in range(PACK_SUB):
        dst_ref[pl.ds(s, rows, stride=PACK_SUB), :] = words[:, s * LANES:(s + 1) * LANES]


def _unpack_words(words):
    lo = lax.bitcast_convert_type(words << 16, F32)
    hi = lax.bitcast_convert_type(words & HIGH_HALF, F32)
    return lo, hi


def _unpack_rows(src_ref, rows):
    los, his = [], []
    for s in range(PACK_SUB):
        lo, hi = _unpack_words(src_ref[pl.ds(s, rows, stride=PACK_SUB), :])
        los.append(lo)
        his.append(hi)
    return jnp.concatenate(los + his, axis=1)


def _route_kernel(x_ref, g_ref, sh_ref, sc_ref, wr_ref, rb_ref,
                  h_ref, e8_ref, pos8_ref, gate8_ref, cnt_ref,
                  carry, tri_tok, tri_exp):
    i = pl.program_id(0)
    tm = ROUTE_TM
    per_group = N_EXPERTS // N_GROUPS

    @pl.when(i == 0)
    def _():
        carry[...] = jnp.zeros(carry.shape, F32)
        r = lax.broadcasted_iota(I32, (tm, tm), 0)
        c = lax.broadcasted_iota(I32, (tm, tm), 1)
        tri_tok[...] = jnp.where(r <= c, 1.0, 0.0).astype(BF16)
        re = lax.broadcasted_iota(I32, (N_EXPERTS, N_EXPERTS), 0)
        ce = lax.broadcasted_iota(I32, (N_EXPERTS, N_EXPERTS), 1)
        tri_exp[...] = jnp.where(ce < re, 1.0, 0.0).astype(BF16)

    h = _norm_modulate(x_ref[...], g_ref[...], sh_ref[0], sc_ref[0])
    _pack_rows(h, h_ref, tm)
    logits = lax.dot_general(wr_ref[...], h, (((1,), (1,)), ((), ())), precision=HIGHEST,
                             preferred_element_type=F32)
    scores = jax.nn.sigmoid(logits)
    sel = scores + rb_ref[...]

    sel3 = sel.reshape(N_GROUPS, per_group, tm)
    within = lax.broadcasted_iota(I32, sel3.shape, 1)
    max1 = jnp.max(sel3, axis=1, keepdims=True)
    first = jnp.min(jnp.where(sel3 == max1, within, per_group), axis=1, keepdims=True)
    max2 = jnp.max(jnp.where(within == first, -jnp.inf, sel3), axis=1, keepdims=True)
    gscore = max1 + max2

    gidx = lax.broadcasted_iota(I32, gscore.shape, 0)
    grank = jnp.zeros(gscore.shape, I32)
    for g in range(N_GROUPS):
        other = gscore[g:g + 1]
        beats = (other > gscore) | ((other == gscore) & (gidx > g))
        grank = grank + jnp.where(beats, 1, 0)
    keep = grank < TOPK_GROUPS
    masked = jnp.where(keep, sel3, NEG_INF).reshape(N_EXPERTS, tm)

    eidx = lax.broadcasted_iota(I32, (N_EXPERTS, tm), 0)
    erank = jnp.zeros((N_EXPERTS, tm), I32)
    for e in range(N_EXPERTS):
        other = masked[e:e + 1, :]
        beats = (other > masked) | ((other == masked) & (eidx > e))
        erank = erank + jnp.where(beats, 1, 0)
    chosen = erank < TOP_K

    gates = jnp.where(chosen, scores, 0.0)
    gates = gates / jnp.sum(gates, axis=0, keepdims=True) * ROUTED_SCALE

    chosen_b = jnp.where(chosen, 1.0, 0.0).astype(BF16)
    slot = jnp.dot(tri_exp[...], chosen_b, preferred_element_type=F32)
    csum = jnp.dot(chosen_b, tri_tok[...], preferred_element_type=F32)
    pos = carry[:, 0:1] + csum - 1.0
    eidx_f = eidx.astype(F32)
    e_rows, p_rows, g_rows = [], [], []
    for kk in range(TOP_K):
        pick = chosen & (slot == float(kk))
        e_rows.append(jnp.sum(jnp.where(pick, eidx_f, 0.0), axis=0, keepdims=True))
        p_rows.append(jnp.sum(jnp.where(pick, pos, 0.0), axis=0, keepdims=True))
        g_rows.append(jnp.sum(jnp.where(pick, gates, 0.0), axis=0, keepdims=True))
    e8_ref[...] = jnp.concatenate(e_rows, axis=0).astype(I32)
    pos8_ref[...] = jnp.concatenate(p_rows, axis=0).astype(I32)
    gate8_ref[...] = jnp.concatenate(g_rows, axis=0)
    total = carry[...] + csum[:, tm - 1:tm]
    carry[...] = total
    cnt_ref[...] = total


def _route(x2, seq, gain, shift, scale, w_router, router_bias):
    n, d = x2.shape
    tm = ROUTE_TM
    bsz = shift.shape[0]
    return pl.pallas_call(
        _route_kernel,
        grid=(n // tm,),
        in_specs=[
            pl.BlockSpec((tm, d), lambda i: (i, 0)),
            pl.BlockSpec((1, d), lambda i: (0, 0)),
            pl.BlockSpec((1, 1, d), lambda i: (i * tm // seq, 0, 0)),
            pl.BlockSpec((1, 1, d), lambda i: (i * tm // seq, 0, 0)),
            pl.BlockSpec((N_EXPERTS, d), lambda i: (0, 0)),
            pl.BlockSpec((N_EXPERTS, 1), lambda i: (0, 0)),
        ],
        out_specs=[
            pl.BlockSpec((tm * PACK_SUB, LANES), lambda i: (i, 0)),
            pl.BlockSpec((TOP_K, tm), lambda i: (0, i)),
            pl.BlockSpec((TOP_K, tm), lambda i: (0, i)),
            pl.BlockSpec((TOP_K, tm), lambda i: (0, i)),
            pl.BlockSpec((N_EXPERTS, LANES), lambda i: (0, 0)),
        ],
        out_shape=[
            jax.ShapeDtypeStruct((n * PACK_SUB, LANES), U32),
            jax.ShapeDtypeStruct((TOP_K, n), I32),
            jax.ShapeDtypeStruct((TOP_K, n), I32),
            jax.ShapeDtypeStruct((TOP_K, n), F32),
            jax.ShapeDtypeStruct((N_EXPERTS, LANES), F32),
        ],
        scratch_shapes=[
            pltpu.VMEM((N_EXPERTS, LANES), F32),
            pltpu.VMEM((tm, tm), BF16),
            pltpu.VMEM((N_EXPERTS, N_EXPERTS), BF16),
        ],
        compiler_params=_cparams(("arbitrary",)),
        name="route",
    )(x2, gain.reshape(1, d), shift.reshape(bsz, 1, d), scale.reshape(bsz, 1, d),
      w_router.T, router_bias.reshape(N_EXPERTS, 1))


def _tables_kernel(cnt_ref, e8_ref, pos8_ref, dest_ref, bexp_ref, ends_ref, *, nb_pad):
    counts = cnt_ref[...]
    blocks = jnp.ceil(counts / float(MOE_BM))
    re = lax.broadcasted_iota(I32, (N_EXPERTS, N_EXPERTS), 0)
    ce = lax.broadcasted_iota(I32, (N_EXPERTS, N_EXPERTS), 1)
    tri = jnp.where(ce < re, 1.0, 0.0).astype(BF16)
    start_blk = jnp.dot(tri, blocks.astype(BF16), preferred_element_type=F32)
    end_blk = start_blk + blocks
    e8 = e8_ref[...]
    dest = pos8_ref[...].astype(F32)
    for e in range(N_EXPERTS):
        dest = dest + jnp.where(e8 == e, start_blk[e:e + 1, 0:1] * float(MOE_BM), 0.0)
    dest_ref[...] = dest.astype(I32)
    bidx = lax.broadcasted_iota(I32, (N_EXPERTS, nb_pad), 1).astype(F32)
    owner = jnp.sum(jnp.where(end_blk[:, 0:1] <= bidx, 1.0, 0.0), axis=0, keepdims=True)
    bexp_ref[...] = jnp.minimum(owner, float(N_EXPERTS - 1)).astype(I32)
    ends_ref[...] = end_blk.astype(I32)


def _tables(counts, e8, pos8, nb):
    n = e8.shape[1]
    tm = 2048
    nb_pad = -(-nb // LANES) * LANES
    kern = functools.partial(_tables_kernel, nb_pad=nb_pad)
    return pl.pallas_call(
        kern,
        grid=(n // tm,),
        in_specs=[
            pl.BlockSpec((N_EXPERTS, LANES), lambda i: (0, 0)),
            pl.BlockSpec((TOP_K, tm), lambda i: (0, i)),
            pl.BlockSpec((TOP_K, tm), lambda i: (0, i)),
        ],
        out_specs=[
            pl.BlockSpec((TOP_K, tm), lambda i: (0, i)),
            pl.BlockSpec((1, nb_pad), lambda i: (0, 0)),
            pl.BlockSpec((N_EXPERTS, LANES), lambda i: (0, 0)),
        ],
        out_shape=[
            jax.ShapeDtypeStruct((TOP_K, n), I32),
            jax.ShapeDtypeStruct((1, nb_pad), I32),
            jax.ShapeDtypeStruct((N_EXPERTS, LANES), I32),
        ],
        compiler_params=_cparams(("arbitrary",)),
        name="tables",
    )(counts, e8, pos8)


def _dispatch_kernel(ends_ref, tab_hbm, h_ref, xs_hbm, tab_smem, zero_buf,
                     sem_tab, sem_rows, sem_zero, *, n_tiles, n_blocks):
    i = pl.program_id(0)
    slot = i % 2
    tm = DISP_TM
    blk_rows = MOE_BM * PACK_SUB

    def tab_copy(t, sl):
        return pltpu.make_async_copy(tab_hbm.at[t], tab_smem.at[sl], sem_tab.at[sl])

    def zero_copy(e):
        end = ends_ref[e]
        start = jnp.where(e > 0, ends_ref[jnp.maximum(e - 1, 0)], 0)
        row0 = pl.multiple_of(jnp.maximum(end - 1, 0) * blk_rows, PACK_SUB)
        return end > start, pltpu.make_async_copy(
            zero_buf, xs_hbm.at[pl.ds(row0, blk_rows), :], sem_zero)

    @pl.when(i == 0)
    def _():
        tab_copy(0, 0).start()
        zero_buf[...] = jnp.zeros(zero_buf.shape, U32)

        def start_body(e, carry):
            has_rows, cp = zero_copy(e)

            @pl.when(has_rows)
            def _():
                cp.start()
            return carry

        def wait_body(e, carry):
            has_rows, cp = zero_copy(e)

            @pl.when(has_rows)
            def _():
                cp.wait()
            return carry

        def tail_copy(blk):
            row0 = pl.multiple_of(blk * blk_rows, PACK_SUB)
            return pltpu.make_async_copy(zero_buf, xs_hbm.at[pl.ds(row0, blk_rows), :],
                                         sem_zero)

        def tail_start(blk, carry):
            tail_copy(blk).start()
            return carry

        def tail_wait(blk, carry):
            tail_copy(blk).wait()
            return carry

        n_used = ends_ref[N_EXPERTS - 1]
        lax.fori_loop(0, N_EXPERTS, start_body, 0)
        lax.fori_loop(n_used, n_blocks, tail_start, 0)
        lax.fori_loop(0, N_EXPERTS, wait_body, 0)
        lax.fori_loop(n_used, n_blocks, tail_wait, 0)

    tab_copy(i, slot).wait()

    @pl.when(i + 1 < n_tiles)
    def _():
        tab_copy(i + 1, 1 - slot).start()

    def body(r, carry):
        src = pl.multiple_of(r * PACK_SUB, PACK_SUB)
        for kk in range(TOP_K):
            dst = pl.multiple_of(tab_smem[slot, kk * tm + r] * PACK_SUB, PACK_SUB)
            pltpu.make_async_copy(h_ref.at[pl.ds(src, PACK_SUB), :],
                                  xs_hbm.at[pl.ds(dst, PACK_SUB), :], sem_rows).start()
        return carry

    lax.fori_loop(0, tm, body, 0, unroll=2)
    for kk in range(TOP_K):
        pltpu.make_async_copy(h_ref, xs_hbm.at[pl.ds(0, tm * PACK_SUB), :], sem_rows).wait()


def _dispatch(ends, tab, h_packed, nb):
    n_tiles = tab.shape[0]
    kern = functools.partial(_dispatch_kernel, n_tiles=n_tiles, n_blocks=nb)
    grid_spec = pltpu.PrefetchScalarGridSpec(
        num_scalar_prefetch=1,
        grid=(n_tiles,),
        in_specs=[pl.BlockSpec(memory_space=pl.ANY),
                  pl.BlockSpec((DISP_TM * PACK_SUB, LANES), lambda i, ends: (i, 0))],
        out_specs=pl.BlockSpec(memory_space=pl.ANY),
        scratch_shapes=[
            pltpu.SMEM((2, TOP_K * DISP_TM), I32),
            pltpu.VMEM((MOE_BM * PACK_SUB, LANES), U32),
            pltpu.SemaphoreType.DMA((2,)),
            pltpu.SemaphoreType.DMA,
            pltpu.SemaphoreType.DMA,
        ],
    )
    return pl.pallas_call(
        kern,
        grid_spec=grid_spec,
        out_shape=jax.ShapeDtypeStruct((nb * MOE_BM * PACK_SUB, LANES), U32),
        compiler_params=_cparams(("arbitrary",)),
        name="dispatch",
    )(ends, tab, h_packed)


def _experts_kernel(bexp_ref, nused_ref, xs_ref, wu_ref, wd_ref, y_ref, wu_bf, wd_bf):
    b = pl.program_id(0)
    nu = nused_ref[0]
    bm = MOE_BM

    prev_expert = bexp_ref[jnp.maximum(b - 1, 0)]

    @pl.when((b == 0) | (bexp_ref[b] != prev_expert))
    def _():
        wu_bf[...] = wu_ref[...].astype(BF16)
        wd_bf[...] = wd_ref[...].astype(BF16)

    @pl.when(b < nu)
    def _():
        x = _unpack_rows(xs_ref, bm).astype(BF16)
        up = jnp.dot(x, wu_bf[...], preferred_element_type=F32)
        act = _silu(up[:, :EXPERT_DIM]) * up[:, EXPERT_DIM:]
        y = jnp.dot(act.astype(BF16), wd_bf[...], preferred_element_type=F32)
        _pack_rows(y, y_ref, bm)

    @pl.when(b >= nu)
    def _():
        y_ref[...] = jnp.zeros(y_ref.shape, U32)


def _experts(bexp, nused, xs, w_up_all, w_down_all, layer):
    bm = MOE_BM
    nb = xs.shape[0] // (bm * PACK_SUB)
    d = w_up_all.shape[2]
    grid_spec = pltpu.PrefetchScalarGridSpec(
        num_scalar_prefetch=2,
        grid=(nb,),
        in_specs=[
            pl.BlockSpec((bm * PACK_SUB, LANES),
                         lambda b, be, nu: (jnp.minimum(b, nu[0] - 1), 0)),
            pl.BlockSpec((None, None, d, 2 * EXPERT_DIM),
                         lambda b, be, nu: (layer, be[b], 0, 0)),
            pl.BlockSpec((None, None, EXPERT_DIM, d),
                         lambda b, be, nu: (layer, be[b], 0, 0)),
        ],
        out_specs=pl.BlockSpec((bm * PACK_SUB, LANES), lambda b, be, nu: (b, 0)),
        scratch_shapes=[
            pltpu.VMEM((d, 2 * EXPERT_DIM), BF16),
            pltpu.VMEM((EXPERT_DIM, d), BF16),
        ],
    )
    return pl.pallas_call(
        _experts_kernel,
        grid_spec=grid_spec,
        out_shape=jax.ShapeDtypeStruct((nb * bm * PACK_SUB, LANES), U32),
        compiler_params=_cparams(("arbitrary",)),
        name="experts",
    )(bexp, nused, xs, w_up_all, w_down_all)


def _combine_kernel(tab_hbm, y_hbm, gate_ref, h_ref, x_ref, gf_ref, wsu_ref, wsd_ref, out_ref,
                    ybuf, tab_smem, shared_scr, sem_tab, sem_rows, *, n_tiles):
    i = pl.program_id(0)
    slot = i % 2
    tm = COMB_TM

    def tab_copy(t, sl):
        return pltpu.make_async_copy(tab_hbm.at[t], tab_smem.at[sl], sem_tab.at[sl])

    def issue_rows(sl):
        for kk in range(TOP_K):
            def body(r, carry, kk=kk):
                src = pl.multiple_of(tab_smem[sl, kk * tm + r] * PACK_SUB, PACK_SUB)
                dst = pl.multiple_of(r * PACK_SUB, PACK_SUB)
                pltpu.make_async_copy(y_hbm.at[pl.ds(src, PACK_SUB), :],
                                      ybuf.at[sl, kk, pl.ds(dst, PACK_SUB), :],
                                      sem_rows.at[sl]).start()
                return carry
            lax.fori_loop(0, tm, body, 0, unroll=8)

    @pl.when(i == 0)
    def _():
        tab_copy(0, 0).start()
        tab_copy(0, 0).wait()
        issue_rows(0)
        if n_tiles > 1:
            tab_copy(1, 1).start()

    @pl.when(i + 1 < n_tiles)
    def _():
        tab_copy(i + 1, 1 - slot).wait()
        issue_rows(1 - slot)

    @pl.when(i + 2 < n_tiles)
    def _():
        tab_copy(i + 2, slot).start()

    hb = _unpack_rows(h_ref, tm).astype(BF16)
    up = jnp.dot(hb, wsu_ref[...], preferred_element_type=F32)
    half = wsd_ref.shape[0]
    act = _silu(up[:, :half]) * up[:, half:]
    shared_scr[...] = jnp.dot(act.astype(BF16), wsd_ref[...], preferred_element_type=F32)

    for kk in range(TOP_K):
        pltpu.make_async_copy(y_hbm.at[pl.ds(0, tm * PACK_SUB), :], ybuf.at[slot, kk],
                              sem_rows.at[slot]).wait()
    gate = gate_ref[...]
    gates = [jnp.broadcast_to(gate[:, kk:kk + 1], (tm, LANES)) for kk in range(TOP_K)]
    d_half = PACK_SUB * LANES
    for s in range(PACK_SUB):
        acc_lo = jnp.zeros((tm, LANES), F32)
        acc_hi = jnp.zeros((tm, LANES), F32)
        for kk in range(TOP_K):
            lo, hi = _unpack_words(ybuf[slot, kk, pl.ds(s, tm, stride=PACK_SUB), :])
            acc_lo = acc_lo + gates[kk] * lo
            acc_hi = acc_hi + gates[kk] * hi
        for acc, c0 in ((acc_lo, s * LANES), (acc_hi, d_half + s * LANES)):
            cs = slice(c0, c0 + LANES)
            out_ref[:, cs] = x_ref[:, cs] + gf_ref[0, :, cs] * (acc + shared_scr[:, cs])


def _combine(tab, y, gate_t, h, x2, g_f, w_su, w_sd, seq):
    n, d = x2.shape
    tm = COMB_TM
    n_tiles = n // tm
    bsz = g_f.shape[0]
    kern = functools.partial(_combine_kernel, n_tiles=n_tiles)
    return pl.pallas_call(
        kern,
        grid=(n_tiles,),
        in_specs=[
            pl.BlockSpec(memory_space=pl.ANY),
            pl.BlockSpec(memory_space=pl.ANY),
            pl.BlockSpec((tm, TOP_K), lambda i: (i, 0)),
            pl.BlockSpec((tm * PACK_SUB, LANES), lambda i: (i, 0)),
            pl.BlockSpec((tm, d), lambda i: (i, 0)),
            pl.BlockSpec((1, 1, d), lambda i: (i * tm // seq, 0, 0)),
            pl.BlockSpec(w_su.shape, lambda i: (0, 0)),
            pl.BlockSpec(w_sd.shape, lambda i: (0, 0)),
        ],
        out_specs=pl.BlockSpec((tm, d), lambda i: (i, 0)),
        out_shape=jax.ShapeDtypeStruct((n, d), F32),
        scratch_shapes=[
            pltpu.VMEM((2, TOP_K, tm * PACK_SUB, LANES), U32),
            pltpu.SMEM((2, TOP_K * tm), I32),
            pltpu.VMEM((tm, d), F32),
            pltpu.SemaphoreType.DMA((2,)),
            pltpu.SemaphoreType.DMA((2,)),
        ],
        compiler_params=_cparams(("arbitrary",)),
        name="combine",
    )(tab, y, gate_t, h, x2, g_f.reshape(bsz, 1, d), w_su, w_sd)


def _tile_table(dest8, tm):
    n = dest8.shape[1]
    return dest8.reshape(TOP_K, n // tm, tm).transpose(1, 0, 2).reshape(n // tm, TOP_K * tm)


def _moe(x2, seq, gain, shift, scale, g_f, w_router, router_bias, w_up_all, w_down_all, layer,
         w_su, w_sd):
    n, d = x2.shape
    assert d == 2 * PACK_SUB * LANES
    h, e8, pos8, gate8, counts = _route(x2, seq, gain, shift, scale, w_router, router_bias)
    nb = -(-(n * TOP_K + N_EXPERTS * (MOE_BM - 1)) // MOE_BM)
    dest8, bexp, ends = _tables(counts, e8, pos8, nb)
    ends = ends[:, 0]
    xs = _dispatch(ends, _tile_table(dest8, DISP_TM), h, nb)
    y = _experts(bexp[0, :nb], ends[N_EXPERTS - 1:], xs, w_up_all, w_down_all, layer)
    return _combine(_tile_table(dest8, COMB_TM), y, gate8.T, h, x2, g_f, w_su.astype(BF16),
                    w_sd.astype(BF16), seq)


def _alibi_slopes(nh):
    return jnp.asarray(2.0 ** (-8.0 * np.arange(1, nh + 1) / nh), dtype=F32)


def _even_weights(w_in, qk_norm):
    a_w = A_HEADS * HEAD_DIM
    b_w = B_HEADS * HEAD_DIM
    kv_w = B_KV_HEADS * HEAD_DIM
    cuts = np.cumsum([a_w, a_w, a_w, b_w, kv_w])
    qa, ka, va, qb, kb, vb = jnp.split(w_in, cuts, axis=1)

    def dup(a):
        return jnp.concatenate([a[:, hh * HEAD_DIM:(hh + 1) * HEAD_DIM]
                                for hh in range(B_KV_HEADS) for _ in range(2)], axis=1)

    w = jnp.concatenate([qa, ka, qb, dup(kb), va, dup(vb)], axis=1).astype(BF16)
    q_scale = HEAD_DIM ** -0.5 * LOG2E
    gains = jnp.concatenate([
        jnp.tile(qk_norm[0] * q_scale, A_HEADS), jnp.tile(qk_norm[1], A_HEADS),
        jnp.tile(qk_norm[2] * q_scale, B_HEADS), jnp.tile(qk_norm[3], 2 * B_KV_HEADS),
        jnp.ones((a_w + 2 * kv_w,), F32)])
    n_norm = 2 * a_w + b_w + 2 * kv_w
    return w, gains, n_norm


def kernel(x, c, norm_mix, norm_ffn, w_ada, b_ada, ab_w_in, ab_qk_norm, ab_sinks, ab_w_out,
           c_w_in, c_qk_norm, c_lambda, c_subln, c_w_out, w_router, router_bias,
           experts_up, experts_down, shared_up, shared_down):
    bsz, seq, d = x.shape
    depth = w_ada.shape[0]
    n = bsz * seq
    ada = _ada(c, w_ada, b_ada)
    x2 = x.reshape(n, d)
    for i in range(depth):
        sh_m, sc_m, g_m, sh_f, sc_f, g_f = jnp.split(ada[i], 6, axis=-1)
        j = i // 2
        if i % 2 == 0:
            w, gains, n_norm = _even_weights(ab_w_in[j], ab_qk_norm[j])
            qkv = _proj(x2, seq, norm_mix[i], sh_m, sc_m, w, gains, n_norm)
            pa = A_HEADS // 2
            pb = B_HEADS // 2
            q_a, k_a, q_b, k_b = 0, pa, 2 * pa, 2 * pa + pb
            v_a = k_b + B_KV_HEADS
            v_b = v_a + pa
            o_a = _attn_a(qkv, bsz, seq, q_a, k_a, v_a, _alibi_slopes(A_HEADS))
            o_b = _attn_b(qkv, bsz, seq, q_b, k_b, v_b, _alibi_slopes(B_HEADS),
                          ab_sinks[j].astype(F32))
            x2 = _outproj(o_a.reshape(n, -1), 0, o_b.reshape(n, -1), 0,
                          ab_w_out[j].astype(BF16), x2, g_m, seq)
        else:
            q_scale = HEAD_DIM ** -0.5 * LOG2E
            gains = jnp.concatenate([
                jnp.tile(c_qk_norm[j, 0] * q_scale, 2 * C_HEADS),
                jnp.tile(c_qk_norm[j, 1], 2 * C_HEADS),
                jnp.ones((2 * C_HEADS * HEAD_DIM,), F32)])
            qkv = _proj(x2, seq, norm_mix[i], sh_m, sc_m, c_w_in[j].astype(BF16), gains,
                        4 * C_HEADS * HEAD_DIM)
            lam_init = 0.8 - 0.6 * math.exp(-0.3 * i)
            o_c = _attn_c(qkv, bsz, seq, _alibi_slopes(C_HEADS), c_qk_norm[j],
                          c_lambda[j].astype(F32),
                          c_subln[j].astype(F32), lam_init).reshape(n, -1)
            x2 = _outproj(o_c, 0, o_c, 1, c_w_out[j].astype(BF16), x2, g_m, seq)
        x2 = _moe(x2, seq, norm_ffn[i], sh_f, sc_f, g_f, w_router[i], router_bias[i],
                  experts_up, experts_down, i, shared_up[i], shared_down[i])
    return x2.reshape(bsz, seq, d)
```
